```python
import math
import jax, jax.numpy as jnp
from jax import lax
import numpy as np


D_MODEL = 1024
BATCH = 8
SEQ = 4096
DEPTH = 4

CTX_LEN = 256
GRID_W = 64
EPS = 1e-6

DA_HEADS = 4
DA_WIDTH = D_MODEL // 2
DA_HEAD_DIM = DA_WIDTH // (2 * DA_HEADS)
ROPE_BASE = 10000.0
Q_BLOCK = 128

CM_GROUPS = 4
CM_WIDTH = D_MODEL // 4
CM_GROUP_DIM = CM_WIDTH // CM_GROUPS
CHUNK = 128

POOL_WINDOWS = (2, 4, 8, 16)
PL_WIDTH = D_MODEL // 4
PL_GROUP_DIM = PL_WIDTH // len(POOL_WINDOWS)

MIX_WIDTH = DA_WIDTH + CM_WIDTH + PL_WIDTH
IN_WIDTH = 3 * DA_WIDTH + 2 * CM_WIDTH + PL_WIDTH
D_FF = 4 * D_MODEL

kernel_name = 'hybrid_diffattn_gmlp_pool_dit_trunk'


def rmsnorm(x, g):
    xf = x.astype(jnp.float32)
    y = xf * lax.rsqrt(jnp.mean(xf * xf, axis=-1, keepdims=True) + EPS)
    return (y * g.astype(jnp.float32)).astype(x.dtype)


def modulation(cond, w_ada, b_ada):
    m = jax.nn.silu(cond) @ w_ada + b_ada
    return jnp.split(m, 6, axis=-1)


def modulate(h, shift, scale):
    return h * (1 + scale) + shift


def axial_rope_tables(n_tokens):
    n_rows = n_tokens // GRID_W
    row = jnp.repeat(jnp.arange(n_rows), GRID_W).astype(jnp.float32)
    col = jnp.tile(jnp.arange(GRID_W), n_rows).astype(jnp.float32)
    n_freq = DA_HEAD_DIM // 4
    inv = ROPE_BASE ** (-jnp.arange(n_freq, dtype=jnp.float32) / n_freq)
    ang = jnp.stack([row[:, None] * inv, col[:, None] * inv], axis=1)
    return jnp.cos(ang), jnp.sin(ang)


def apply_axial_rope(x, cos, sin):
    n_freq = x.shape[-1] // 4
    xs = x.astype(jnp.float32).reshape(x.shape[:-1] + (2, 2, n_freq))
    x1 = xs[..., 0, :]
    x2 = xs[..., 1, :]
    c = cos[None, :, None, None]
    s = sin[None, :, None, None]
    out = jnp.stack([x1 * c - x2 * s, x2 * c + x1 * s], axis=-2)
    return out.reshape(x.shape).astype(x.dtype)


def diff_attention(q, k, v, lam):
    scale = DA_HEAD_DIM ** -0.5
    s = jnp.einsum('bqhjd,bkhjd->bhjqk', q, k).astype(jnp.float32) * scale
    p = jax.nn.softmax(s, axis=-1)
    a = p[:, :, 0] - lam * p[:, :, 1]
    return jnp.einsum('bhqk,bkhe->bqhe', a.astype(v.dtype), v)


def latent_diff_attention(ql, kl, vl, kc, vc, lam):
    b, t, h, _, d = ql.shape
    k = jnp.concatenate([kc, kl], axis=1)
    v = jnp.concatenate([vc, vl], axis=1)
    nb = t // Q_BLOCK
    qb = jnp.moveaxis(ql.reshape(b, nb, Q_BLOCK, h, 2, d), 1, 0)
    out = lax.map(lambda qblk: diff_attention(qblk, k, v, lam), qb)
    return jnp.moveaxis(out, 0, 1).reshape(b, t, h, 2 * d)


def chunk_spatial_gating(u, v, g_v, w_s, b_s):
    b, t, _ = v.shape
    n = t // CHUNK
    vn = rmsnorm(v.reshape(b, t, CM_GROUPS, CM_GROUP_DIM), g_v.reshape(CM_GROUPS, CM_GROUP_DIM))
    vn = vn.reshape(b, n, CHUNK, CM_GROUPS, CM_GROUP_DIM)
    vm = jnp.einsum('gpq,bnqgc->bnpgc', w_s, vn) + b_s.T[:, :, None]
    return u * vm.reshape(b, t, CM_WIDTH)


def multiscale_pool(x, w_pool, s_pool):
    b, t, _ = x.shape
    xg = x.reshape(b, t, len(POOL_WINDOWS), PL_GROUP_DIM)
    pos = jnp.arange(t)
    outs = []
    for gi, w in enumerate(POOL_WINDOWS):
        xs = xg[:, :, gi].astype(jnp.float32)
        cs = jnp.concatenate([jnp.zeros_like(xs[:, :1]), jnp.cumsum(xs, axis=1)], axis=1)
        lo = jnp.clip(pos - w // 2, 0, t)
        hi = jnp.clip(pos + (w - w // 2), 0, t)
        mean = (cs[:, hi] - cs[:, lo]) / (hi - lo).astype(jnp.float32)[None, :, None]
        outs.append(mean - xs)
    d = jnp.stack(outs, axis=2).astype(x.dtype)
    y = jnp.einsum('btgc,gce->btge', d, w_pool).reshape(b, t, PL_WIDTH)
    return y * s_pool


def split_projection(h, w_in):
    z = h @ w_in
    b, t, _ = z.shape
    cuts = np.cumsum([DA_WIDTH, DA_WIDTH, DA_WIDTH, CM_WIDTH, CM_WIDTH])
    q, k, v, u, gv, p = jnp.split(z, [int(i) for i in cuts], axis=-1)
    q = q.reshape(b, t, DA_HEADS, 2, DA_HEAD_DIM)
    k = k.reshape(b, t, DA_HEADS, 2, DA_HEAD_DIM)
    v = v.reshape(b, t, DA_HEADS, 2 * DA_HEAD_DIM)
    return q, k, v, u, gv, p


def merge_heads(a, g_sub, lam_init, u, gv, p, g_v, w_s, b_s, w_pool, s_pool, w_out):
    b, t = a.shape[:2]
    a = (rmsnorm(a, g_sub) * (1 - lam_init)).reshape(b, t, DA_WIDTH)
    m_b = chunk_spatial_gating(u, gv, g_v, w_s, b_s)
    m_c = multiscale_pool(p, w_pool, s_pool)
    return jnp.concatenate([a, m_b, m_c], axis=-1) @ w_out


def sq_relu_mlp(h, w1, w2):
    return jnp.square(jax.nn.relu(h @ w1)) @ w2


def setup_inputs(seed: int = 0) -> dict:
    key = jax.random.key(seed)
    ks = jax.random.split(key, 24)
    f32 = jnp.float32
    nrm = lambda k, shape: jax.random.normal(k, shape, f32)
    L = DEPTH
    return {
        'x': nrm(ks[0], (BATCH, SEQ, D_MODEL)),
        'c': nrm(ks[1], (BATCH, D_MODEL)),
        'ctx': nrm(ks[2], (BATCH, CTX_LEN, D_MODEL)),
        'c_ctx': nrm(ks[3], (D_MODEL,)),
        'w_ada': nrm(ks[4], (L, D_MODEL, 6 * D_MODEL)) * (0.5 * D_MODEL ** -0.5),
        'b_ada': nrm(ks[5], (L, 6 * D_MODEL)) * 0.02,
        'g_norm_mix': 1.0 + 0.02 * nrm(ks[6], (L, D_MODEL)),
        'g_norm_mlp': 1.0 + 0.02 * nrm(ks[7], (L, D_MODEL)),
        'w_in': nrm(ks[8], (L, D_MODEL, IN_WIDTH)) * D_MODEL ** -0.5,
        'lam_q1': 0.1 * nrm(ks[9], (L, DA_HEAD_DIM)),
        'lam_k1': 0.1 * nrm(ks[10], (L, DA_HEAD_DIM)),
        'lam_q2': 0.1 * nrm(ks[11], (L, DA_HEAD_DIM)),
        'lam_k2': 0.1 * nrm(ks[12], (L, DA_HEAD_DIM)),
        'g_subln': 1.0 + 0.02 * nrm(ks[13], (L, 2 * DA_HEAD_DIM)),
        'g_vnorm': 1.0 + 0.02 * nrm(ks[14], (L, CM_WIDTH)),
        'w_spatial': nrm(ks[15], (L, CM_GROUPS, CHUNK, CHUNK)) * CHUNK ** -0.5,
        'b_spatial': 1.0 + 0.02 * nrm(ks[16], (L, CM_GROUPS, CHUNK)),
        'w_pool': nrm(ks[17], (L, len(POOL_WINDOWS), PL_GROUP_DIM, PL_GROUP_DIM)) * PL_GROUP_DIM ** -0.5,
        's_pool': 1.0 + 0.02 * nrm(ks[18], (L, PL_WIDTH)),
        'w_out': nrm(ks[19], (L, MIX_WIDTH, D_MODEL)) * MIX_WIDTH ** -0.5,
        'w1': nrm(ks[20], (L, D_MODEL, D_FF)) * D_MODEL ** -0.5,
        'w2': nrm(ks[21], (L, D_FF, D_MODEL)) * D_FF ** -0.5,
        'g_final': 1.0 + 0.02 * nrm(ks[22], (D_MODEL,)),
    }


def reference(x, c, ctx, c_ctx, w_ada, b_ada, g_norm_mix, g_norm_mlp, w_in,
              lam_q1, lam_k1, lam_q2, lam_k2, g_subln, g_vnorm, w_spatial, b_spatial,
              w_pool, s_pool, w_out, w1, w2, g_final):
    xl = x
    xc = ctx
    cos, sin = axial_rope_tables(xl.shape[1])
    for l in range(DEPTH):
        last = l == DEPTH - 1
        lam_init = 0.8 - 0.6 * math.exp(-0.3 * l)
        lam = (jnp.exp(jnp.sum(lam_q1[l].astype(jnp.float32) * lam_k1[l].astype(jnp.float32)))
               - jnp.exp(jnp.sum(lam_q2[l].astype(jnp.float32) * lam_k2[l].astype(jnp.float32)))
               + lam_init)
        sh1, sc1, gt1, sh2, sc2, gt2 = [m[:, None, :] for m in modulation(c, w_ada[l], b_ada[l])]
        csh1, csc1, cgt1, csh2, csc2, cgt2 = modulation(c_ctx, w_ada[l], b_ada[l])

        hl = modulate(rmsnorm(xl, g_norm_mix[l]), sh1, sc1)
        hc = modulate(rmsnorm(xc, g_norm_mix[l]), csh1, csc1)
        ql, kl, vl, ul, gvl, pl = split_projection(hl, w_in[l])
        qc, kc, vc, uc, gvc, pc = split_projection(hc, w_in[l])
        ql = apply_axial_rope(ql, cos, sin)
        kl = apply_axial_rope(kl, cos, sin)
        al = latent_diff_attention(ql, kl, vl, kc, vc, lam)
        mix_l = merge_heads(al, g_subln[l], lam_init, ul, gvl, pl, g_vnorm[l], w_spatial[l],
                            b_spatial[l], w_pool[l], s_pool[l], w_out[l])
        xl = xl + gt1 * mix_l
        if not last:
            ac = diff_attention(qc, kc, vc, lam)
            mix_c = merge_heads(ac, g_subln[l], lam_init, uc, gvc, pc, g_vnorm[l], w_spatial[l],
                                b_spatial[l], w_pool[l], s_pool[l], w_out[l])
            xc = xc + cgt1 * mix_c

        hl = modulate(rmsnorm(xl, g_norm_mlp[l]), sh2, sc2)
        xl = xl + gt2 * sq_relu_mlp(hl, w1[l], w2[l])
        if not last:
            hc = modulate(rmsnorm(xc, g_norm_mlp[l]), csh2, csc2)
            xc = xc + cgt2 * sq_relu_mlp(hc, w1[l], w2[l])
    return rmsnorm(xl, g_final)
```

```python
import functools
import math

import jax
import jax.numpy as jnp
from jax import lax
from jax.experimental import pallas as pl
from jax.experimental.pallas import tpu as pltpu

EPS = 1e-6
GRID_W = 64
ROPE_BASE = 10000.0
DA_HEADS = 4
POOL_WINDOWS = (2, 4, 8, 16)
CM_GROUPS = 4
CHUNK = 128

LANES = 128
SUBLANES = 8
BF16_ROWS = 16
VMEM_LIMIT = 56 * 1024 * 1024
POOL_HALO = 8

TM_LATENT = 512
TQ_LATENT = 512
TK_LATENT = 1024
TF_MLP = 1024

F32 = jnp.float32
BF16 = jnp.bfloat16


def _params(n_axes):
    return pltpu.CompilerParams(
        dimension_semantics=("arbitrary",) * n_axes, vmem_limit_bytes=VMEM_LIMIT)


def _resident(shape):
    zeros = (0,) * len(shape)
    return pl.BlockSpec(shape, lambda *_: zeros, pipeline_mode=pl.Buffered(1))


def _rmsnorm_rows(x, g):
    ms = jnp.mean(x * x, axis=-1, keepdims=True)
    return x * lax.rsqrt(ms + EPS) * g


def _ada_body(cond_ref, w_ref, b_ref, o_ref):
    c = cond_ref[...]
    s = c * (1.0 / (1.0 + jnp.exp(-c)))
    o_ref[0] = jnp.dot(s, w_ref[0], preferred_element_type=F32,
                       precision=lax.Precision.HIGHEST) + b_ref[0]


def _ada_all_layers(conds, w_ada, b_ada):
    n_layers, d, n_out = w_ada.shape
    rows = conds.shape[0]
    tn = n_out // 4
    return pl.pallas_call(
        _ada_body,
        grid=(n_layers, n_out // tn),
        in_specs=[
            pl.BlockSpec((rows, d), lambda l, j: (0, 0)),
            pl.BlockSpec((1, d, tn), lambda l, j: (l, 0, j)),
            pl.BlockSpec((1, 1, tn), lambda l, j: (l, 0, j)),
        ],
        out_specs=pl.BlockSpec((1, rows, tn), lambda l, j: (l, 0, j)),
        out_shape=jax.ShapeDtypeStruct((n_layers, rows, n_out), F32),
        compiler_params=_params(2),
        name="ada",
    )(conds, w_ada, b_ada.reshape(n_layers, 1, n_out))


def _swap16(z):
    up = pltpu.roll(z, LANES - 16, axis=1)
    down = pltpu.roll(z, 16, axis=1)
    lane = lax.broadcasted_iota(jnp.int32, z.shape, 1)
    return jnp.where((lane & 16) == 0, up, down)


def _inproj_body(da, rope, x_ref, mod_ref, g_ref, w_ref, *rest):
    if rope:
        cos_ref, sin_ref, qt_ref, k_ref, vt_ref, ugp_ref = rest
    else:
        qt_ref, k_ref, vt_ref, ugp_ref = rest
    x = x_ref[0]
    shift = mod_ref[0, 0:1, :]
    scale = mod_ref[0, 1:2, :]
    h = (_rmsnorm_rows(x, g_ref[...]) * (1.0 + scale) + shift).astype(BF16)
    z = jnp.dot(h, w_ref[...], preferred_element_type=F32)

    def rotated(lo):
        cols = []
        for c in range(da // LANES):
            zc = z[:, lo + c * LANES: lo + (c + 1) * LANES]
            if rope:
                zc = zc * cos_ref[...] + _swap16(zc) * sin_ref[...]
            cols.append(zc)
        return jnp.concatenate(cols, axis=1)

    head_dim = da // (2 * DA_HEADS)
    q = rotated(0) * (head_dim ** -0.5)
    qt_ref[0] = q.T.astype(BF16)
    k_ref[0] = rotated(da).astype(BF16)
    vt_ref[0] = z[:, 2 * da: 3 * da].T.astype(BF16)
    ugp_ref[0] = z[:, 3 * da:]


def _inproj(x, mod, g, w_bf16, rope_tabs, tm):
    b, s, d = x.shape
    in_width = w_bf16.shape[1]
    da = d // 2
    rest = in_width - 3 * da
    per_batch_mod = mod.shape[0] > 1
    rope = rope_tabs is not None
    in_specs = [
        pl.BlockSpec((1, tm, d), lambda bi, i: (bi, i, 0)),
        pl.BlockSpec((1, 6, d), (lambda bi, i: (bi, 0, 0)) if per_batch_mod else (lambda bi, i: (0, 0, 0))),
        _resident((1, d)),
        _resident((d, in_width)),
    ]
    args = [x, mod, g, w_bf16]
    if rope:
        in_specs += [pl.BlockSpec((tm, LANES), lambda bi, i: (i, 0))] * 2
        args += list(rope_tabs)
    return pl.pallas_call(
        functools.partial(_inproj_body, da, rope),
        grid=(b, s // tm),
        in_specs=in_specs,
        out_specs=[
            pl.BlockSpec((1, da, tm), lambda bi, i: (bi, 0, i)),
            pl.BlockSpec((1, tm, da), lambda bi, i: (bi, i, 0)),
            pl.BlockSpec((1, da, tm), lambda bi, i: (bi, 0, i)),
            pl.BlockSpec((1, tm, rest), lambda bi, i: (bi, i, 0)),
        ],
        out_shape=[
            jax.ShapeDtypeStruct((b, da, s), BF16),
            jax.ShapeDtypeStruct((b, s, da), BF16),
            jax.ShapeDtypeStruct((b, da, s), BF16),
            jax.ShapeDtypeStruct((b, s, rest), F32),
        ],
        compiler_params=_params(2),
        name="inproj_rope" if rope else "inproj",
    )(*args)


def _attn_body(n_lat, tk, lam_init, *refs):
    if n_lat:
        (qt_ref, kc_ref, vtc_ref, kl_ref, vtl_ref, lamv_ref, gsub_ref,
         o_ref, vec_ref, vel_ref, m_ref, acc_ref) = refs
    else:
        (qt_ref, kc_ref, vtc_ref, lamv_ref, gsub_ref,
         o_ref, vec_ref, m_ref, acc_ref) = refs
    hd2 = qt_ref.shape[1]
    tq = qt_ref.shape[2]

    @pl.when(pl.program_id(2) == 0)
    def _():
        vec_ref[0:hd2, :] = vtc_ref[0]
        vec_ref[hd2:, :] = jnp.ones((BF16_ROWS, vec_ref.shape[1]), BF16)
        if n_lat:
            for t in range(n_lat):
                vel_ref[t, 0:hd2, :] = vtl_ref[0, :, t * tk:(t + 1) * tk]
                vel_ref[t, hd2:, :] = jnp.ones((BF16_ROWS, tk), BF16)

    qt = qt_ref[0]
    row = lax.broadcasted_iota(jnp.int32, qt.shape, 0)
    zero = jnp.zeros_like(qt)
    q_comp = (jnp.where(row < hd2 // 2, qt, zero), jnp.where(row >= hd2 // 2, qt, zero))

    m_ref[...] = jnp.full(m_ref.shape, -jnp.inf, F32)
    acc_ref[...] = jnp.zeros(acc_ref.shape, F32)

    def step(k_blk, ve_blk):
        for j in range(2):
            st = jnp.dot(k_blk, q_comp[j], preferred_element_type=F32)
            m_prev = m_ref[j, 0:1, :]
            m_new = jnp.maximum(m_prev, jnp.max(st, axis=0, keepdims=True))
            alpha = jnp.exp(m_prev - m_new)
            pt = jnp.exp(st - m_new).astype(BF16)
            acc_ref[j] = alpha * acc_ref[j] + jnp.dot(ve_blk, pt, preferred_element_type=F32)
            m_ref[j, 0:1, :] = m_new

    step(kc_ref[0], vec_ref[...])
    if n_lat:
        def body(t, carry):
            start = pl.multiple_of(t * tk, tk)
            step(kl_ref[0, pl.ds(start, tk), :], vel_ref[t])
            return carry
        lax.fori_loop(0, n_lat, body, 0)

    lamv = lamv_ref[...]
    lam = (jnp.exp(jnp.sum(lamv[0:1] * lamv[1:2], axis=-1, keepdims=True))
           - jnp.exp(jnp.sum(lamv[2:3] * lamv[3:4], axis=-1, keepdims=True)) + lam_init)
    a0 = acc_ref[0]
    a1 = acc_ref[1]
    o = a0[0:hd2] * (1.0 / a0[hd2:hd2 + 1]) - lam * (a1[0:hd2] * (1.0 / a1[hd2:hd2 + 1]))
    ms = jnp.mean(o * o, axis=0, keepdims=True)
    on = o * lax.rsqrt(ms + EPS) * gsub_ref[...] * (1.0 - lam_init)
    o_ref[0] = on.T.astype(o_ref.dtype)


def _attention(qt, k_c, vt_c, k_l, vt_l, lamv, gsub_col, lam_init, tq, tk):
    b, da, sq = qt.shape
    hd2 = da // DA_HEADS
    sc = k_c.shape[1]
    latent = k_l is not None
    n_lat = k_l.shape[1] // tk if latent else 0
    in_specs = [
        pl.BlockSpec((1, hd2, tq), lambda bi, h, i: (bi, h, i)),
        pl.BlockSpec((1, sc, hd2), lambda bi, h, i: (bi, 0, h)),
        pl.BlockSpec((1, hd2, sc), lambda bi, h, i: (bi, h, 0)),
    ]
    args = [qt, k_c, vt_c]
    scratch = [pltpu.VMEM((hd2 + BF16_ROWS, sc), BF16)]
    if latent:
        sl = k_l.shape[1]
        in_specs += [
            pl.BlockSpec((1, sl, hd2), lambda bi, h, i: (bi, 0, h)),
            pl.BlockSpec((1, hd2, sl), lambda bi, h, i: (bi, h, 0)),
        ]
        args += [k_l, vt_l]
        scratch.append(pltpu.VMEM((n_lat, hd2 + BF16_ROWS, tk), BF16))
    in_specs += [_resident(lamv.shape), _resident(gsub_col.shape)]
    args += [lamv, gsub_col]
    scratch += [pltpu.VMEM((2, SUBLANES, tq), F32), pltpu.VMEM((2, hd2 + BF16_ROWS, tq), F32)]
    return pl.pallas_call(
        functools.partial(_attn_body, n_lat, tk, lam_init),
        grid=(b, DA_HEADS, sq // tq),
        in_specs=in_specs,
        out_specs=pl.BlockSpec((1, tq, hd2), lambda bi, h, i: (bi, i, h)),
        out_shape=jax.ShapeDtypeStruct((b, sq, da), BF16),
        scratch_shapes=scratch,
        compiler_params=_params(3),
        name="attn_latent" if latent else "attn_ctx",
    )(*args)


def _merge_body(seq_len, a_ref, u_ref, gv_ref, p_ref, pprev_ref, pnext_ref, x_ref, mod_ref,
                gvn_ref, ws_ref, bs_ref, gones_ref, wpool_ref, spool_ref, wout_ref, o_ref):
    i = pl.program_id(1)
    tm = u_ref.shape[1]
    cm = u_ref.shape[2]
    gdim = cm // CM_GROUPS
    lane = lax.broadcasted_iota(jnp.int32, (tm, cm), 1)
    grp = lane // gdim

    gv = gv_ref[0]
    gms = jnp.dot(gv * gv, gones_ref[...], preferred_element_type=F32,
                  precision=lax.Precision.HIGHEST)
    vn = (gv * lax.rsqrt(gms + EPS) * gvn_ref[...]).astype(BF16)
    grp_chunk = lax.broadcasted_iota(jnp.int32, (CHUNK, cm), 1) // gdim
    chunks = []
    for c in range(tm // CHUNK):
        vn_c = vn[c * CHUNK:(c + 1) * CHUNK, :]
        vm_c = jnp.zeros((CHUNK, cm), F32)
        for g in range(CM_GROUPS):
            r = jnp.dot(ws_ref[g], vn_c, preferred_element_type=F32)
            vm_c = jnp.where(grp_chunk == g, r, vm_c)
        chunks.append(vm_c + bs_ref[...])
    m_b = u_ref[0] * jnp.concatenate(chunks, axis=0)

    pc = p_ref[0]
    pprev = jnp.where(i == 0, 0.0, pprev_ref[0])
    pnext = jnp.where(i == pl.num_programs(1) - 1, 0.0, pnext_ref[0])
    pe = jnp.concatenate([pprev, pc, pnext], axis=0)

    def sh(o):
        return pe[POOL_HALO + o: POOL_HALO + o + tm, :]

    sums = []
    acc = sh(0)
    half = 0
    for w in POOL_WINDOWS:
        for o in range(-(w // 2), -half):
            acc = acc + sh(o)
        for o in range(max(half, 1), w - w // 2):
            acc = acc + sh(o)
        half = w // 2
        sums.append(acc)
    wsum = sums[-1]
    for g in range(len(POOL_WINDOWS) - 2, -1, -1):
        wsum = jnp.where(grp == g, sums[g], wsum)
    pos = i * tm + lax.broadcasted_iota(jnp.int32, (tm, cm), 0)
    wlo = jnp.full((tm, cm), POOL_WINDOWS[-1] // 2, jnp.int32)
    for g in range(len(POOL_WINDOWS) - 2, -1, -1):
        wlo = jnp.where(grp == g, POOL_WINDOWS[g] // 2, wlo)
    cnt = jnp.minimum(pos + wlo, seq_len) - jnp.maximum(pos - wlo, 0)
    d_pool = (wsum / cnt.astype(F32) - pc).astype(BF16)
    m_c = jnp.dot(d_pool, wpool_ref[...], preferred_element_type=F32) * spool_ref[...]

    mix = jnp.concatenate([a_ref[0], m_b.astype(BF16), m_c.astype(BF16)], axis=1)
    o = jnp.dot(mix, wout_ref[...], preferred_element_type=F32)
    o_ref[0] = x_ref[0] + mod_ref[0, 2:3, :] * o


def _merge(a, ugp, x, mod, gvn, ws_bf16, bs_tab, gones, wpool_bd, spool, wout_bf16, tm):
    b, s, d = x.shape
    da = a.shape[2]
    cm = ugp.shape[2] // 3
    per_batch_mod = mod.shape[0] > 1
    hb = tm // POOL_HALO
    n_hb = s // POOL_HALO
    in_specs = [
        pl.BlockSpec((1, tm, da), lambda bi, i: (bi, i, 0)),
        pl.BlockSpec((1, tm, cm), lambda bi, i: (bi, i, 0)),
        pl.BlockSpec((1, tm, cm), lambda bi, i: (bi, i, 1)),
        pl.BlockSpec((1, tm, cm), lambda bi, i: (bi, i, 2)),
        pl.BlockSpec((1, POOL_HALO, cm), lambda bi, i: (bi, jnp.maximum(i * hb - 1, 0), 2)),
        pl.BlockSpec((1, POOL_HALO, cm), lambda bi, i: (bi, jnp.minimum((i + 1) * hb, n_hb - 1), 2)),
        pl.BlockSpec((1, tm, d), lambda bi, i: (bi, i, 0)),
        pl.BlockSpec((1, 6, d), (lambda bi, i: (bi, 0, 0)) if per_batch_mod else (lambda bi, i: (0, 0, 0))),
        _resident(gvn.shape), _resident(ws_bf16.shape), _resident(bs_tab.shape), _resident(gones.shape),
        _resident(wpool_bd.shape), _resident(spool.shape), _resident(wout_bf16.shape),
    ]
    return pl.pallas_call(
        functools.partial(_merge_body, s),
        grid=(b, s // tm),
        in_specs=in_specs,
        out_specs=pl.BlockSpec((1, tm, d), lambda bi, i: (bi, i, 0)),
        out_shape=jax.ShapeDtypeStruct((b, s, d), F32),
        compiler_params=_params(2),
        name="merge",
    )(a, ugp, ugp, ugp, ugp, ugp, x, mod, gvn, ws_bf16, bs_tab, gones, wpool_bd, spool, wout_bf16)


def _mlp_body(tf, final, x_ref, mod_ref, g_ref, w1_ref, w2_ref, *rest):
    if final:
        gf_ref, o_ref = rest
    else:
        (o_ref,) = rest
    x = x_ref[0]
    shift = mod_ref[0, 3:4, :]
    scale = mod_ref[0, 4:5, :]
    h = (_rmsnorm_rows(x, g_ref[...]) * (1.0 + scale) + shift).astype(BF16)
    acc = jnp.zeros(x.shape, F32)
    for f in range(w1_ref.shape[1] // tf):
        t = jnp.dot(h, w1_ref[:, f * tf:(f + 1) * tf], preferred_element_type=F32)
        t = jnp.square(jnp.maximum(t, 0.0)).astype(BF16)
        acc = acc + jnp.dot(t, w2_ref[f * tf:(f + 1) * tf, :], preferred_element_type=F32)
    y = x + mod_ref[0, 5:6, :] * acc
    if final:
        y = _rmsnorm_rows(y, gf_ref[...])
    o_ref[0] = y


def _mlp(x, mod, g, w1_bf16, w2_bf16, g_final, tm, tf):
    b, s, d = x.shape
    per_batch_mod = mod.shape[0] > 1
    final = g_final is not None
    in_specs = [
        pl.BlockSpec((1, tm, d), lambda bi, i: (bi, i, 0)),
        pl.BlockSpec((1, 6, d), (lambda bi, i: (bi, 0, 0)) if per_batch_mod else (lambda bi, i: (0, 0, 0))),
        _resident(g.shape), _resident(w1_bf16.shape), _resident(w2_bf16.shape),
    ]
    args = [x, mod, g, w1_bf16, w2_bf16]
    if final:
        in_specs.append(_resident(g_final.shape))
        args.append(g_final)
    return pl.pallas_call(
        functools.partial(_mlp_body, tf, final),
        grid=(b, s // tm),
        in_specs=in_specs,
        out_specs=pl.BlockSpec((1, tm, d), lambda bi, i: (bi, i, 0)),
        out_shape=jax.ShapeDtypeStruct((b, s, d), F32),
        compiler_params=_params(2),
        name="mlp_final" if final else "mlp",
    )(*args)


def _rope_tables(n_tokens, head_dim):
    n_rows = n_tokens // GRID_W
    row = jnp.repeat(jnp.arange(n_rows), GRID_W).astype(F32)
    col = jnp.tile(jnp.arange(GRID_W), n_rows).astype(F32)
    n_freq = head_dim // 4
    inv = ROPE_BASE ** (-jnp.arange(n_freq, dtype=F32) / n_freq)
    cr, sr = jnp.cos(row[:, None] * inv), jnp.sin(row[:, None] * inv)
    cc, sc = jnp.cos(col[:, None] * inv), jnp.sin(col[:, None] * inv)
    cos64 = jnp.concatenate([cr, cr, cc, cc], axis=1)
    sin64 = jnp.concatenate([-sr, sr, -sc, sc], axis=1)
    reps = LANES // head_dim
    return jnp.tile(cos64, (1, reps)), jnp.tile(sin64, (1, reps))


def _block_diag(blocks):
    g, r, c = blocks.shape
    out = jnp.zeros((g * r, g * c), blocks.dtype)
    for i in range(g):
        out = out.at[i * r:(i + 1) * r, i * c:(i + 1) * c].set(blocks[i])
    return out


def kernel(x, c, ctx, c_ctx, w_ada, b_ada, g_norm_mix, g_norm_mlp, w_in, lam_q1, lam_k1, lam_q2,
           lam_k2, g_subln, g_vnorm, w_spatial, b_spatial, w_pool, s_pool, w_out, w1, w2, g_final):
    n_layers = w_ada.shape[0]
    batch, seq, d = x.shape
    ctx_len = ctx.shape[1]
    da = d // 2
    head_dim = da // (2 * DA_HEADS)
    cm = d // 4
    gdim = cm // CM_GROUPS

    tm_l, tm_c = min(TM_LATENT, seq), ctx_len
    tq_l, tq_c = min(TQ_LATENT, seq), ctx_len
    tk = min(TK_LATENT, seq)
    tf = TF_MLP

    rows = -(-(batch + 1) // SUBLANES) * SUBLANES
    conds = jnp.zeros((rows, d), F32).at[:batch].set(c).at[batch].set(c_ctx)
    mods = _ada_all_layers(conds, w_ada, b_ada)
    rope_tabs = _rope_tables(seq, head_dim)
    gones = _block_diag(jnp.full((CM_GROUPS, gdim, gdim), 1.0 / gdim, F32))

    xl, xc = x, ctx
    for l in range(n_layers):
        last = l == n_layers - 1
        lam_init = 0.8 - 0.6 * math.exp(-0.3 * l)
        mod_l = mods[l, :batch].reshape(batch, 6, d)
        mod_c = mods[l, batch].reshape(1, 6, d)
        w_in_b = w_in[l].astype(BF16)
        g_mix = g_norm_mix[l][None]
        lamv = jnp.stack([lam_q1[l], lam_k1[l], lam_q2[l], lam_k2[l]]).astype(F32)
        gsub_col = g_subln[l][:, None]
        merge_w = (g_vnorm[l][None], w_spatial[l].astype(BF16),
                   jnp.repeat(b_spatial[l].T, gdim, axis=1), gones,
                   _block_diag(w_pool[l]).astype(BF16), s_pool[l][None], w_out[l].astype(BF16))
        g_mlp = g_norm_mlp[l][None]
        w1_b, w2_b = w1[l].astype(BF16), w2[l].astype(BF16)

        qt_l, k_l, vt_l, ugp_l = _inproj(xl, mod_l, g_mix, w_in_b, rope_tabs, tm_l)
        qt_c, k_c, vt_c, ugp_c = _inproj(xc, mod_c, g_mix, w_in_b, None, tm_c)
        a_l = _attention(qt_l, k_c, vt_c, k_l, vt_l, lamv, gsub_col, lam_init, tq_l, tk)
        xl = _merge(a_l, ugp_l, xl, mod_l, *merge_w, tm_l)
        xl = _mlp(xl, mod_l, g_mlp, w1_b, w2_b, g_final[None] if last else None, tm_l, tf)
        if not last:
            a_c = _attention(qt_c, k_c, vt_c, None, None, lamv, gsub_col, lam_init, tq_c, tk)
            xc = _merge(a_c, ugp_c, xc, mod_c, *merge_w, tm_c)
            xc = _mlp(xc, mod_c, g_mlp, w1_b, w2_b, None, tm_c, tf)
    return xl
```

```python
import functools
import math

import jax
import jax.numpy as jnp
from jax import lax
from jax.experimental import pallas as pl
from jax.experimental.pallas import tpu as pltpu

EPS = 1e-6
GRID_W = 64
ROPE_BASE = 10000.0
DA_HEADS = 4
POOL_WINDOWS = (2, 4, 8, 16)
CM_GROUPS = 4
CHUNK = 128

LANES = 128
SUBLANES = 8
BF16_ROWS = 16
VMEM_LIMIT = 56 * 1024 * 1024
POOL_HALO = 8

TM_LATENT = 512
TQ_LATENT = 512
TK_LATENT = 1024
TF_MLP = 1024

LOG2_E = math.log2(math.e)
MAX_LAZY_GAP = 64.0

F32 = jnp.float32
BF16 = jnp.bfloat16


def _params(n_axes):
    return pltpu.CompilerParams(
        dimension_semantics=("arbitrary",) * n_axes, vmem_limit_bytes=VMEM_LIMIT)


def _resident(shape):
    zeros = (0,) * len(shape)
    return pl.BlockSpec(shape, lambda *_: zeros, pipeline_mode=pl.Buffered(1))


def _rmsnorm_rows(x, g):
    ms = jnp.mean(x * x, axis=-1, keepdims=True)
    return x * lax.rsqrt(ms + EPS) * g


def _ada_body(cond_ref, w_ref, b_ref, o_ref):
    c = cond_ref[...]
    s = c * (1.0 / (1.0 + jnp.exp(-c)))
    o_ref[0] = jnp.dot(s, w_ref[0], preferred_element_type=F32,
                       precision=lax.Precision.HIGHEST) + b_ref[0]


def _ada_all_layers(conds, w_ada, b_ada):
    n_layers, d, n_out = w_ada.shape
    rows = conds.shape[0]
    tn = n_out // 4
    return pl.pallas_call(
        _ada_body,
        grid=(n_layers, n_out // tn),
        in_specs=[
            pl.BlockSpec((rows, d), lambda l, j: (0, 0)),
            pl.BlockSpec((1, d, tn), lambda l, j: (l, 0, j)),
            pl.BlockSpec((1, 1, tn), lambda l, j: (l, 0, j)),
        ],
        out_specs=pl.BlockSpec((1, rows, tn), lambda l, j: (l, 0, j)),
        out_shape=jax.ShapeDtypeStruct((n_layers, rows, n_out), F32),
        compiler_params=_params(2),
        name="ada",
    )(conds, w_ada, b_ada.reshape(n_layers, 1, n_out))


def _swap16(z):
    up = pltpu.roll(z, LANES - 16, axis=1)
    down = pltpu.roll(z, 16, axis=1)
    lane = lax.broadcasted_iota(jnp.int32, z.shape, 1)
    return jnp.where((lane & 16) == 0, up, down)


def _inproj_body(da, rope, x_ref, mod_ref, g_ref, w_ref, *rest):
    if rope:
        cos_ref, sin_ref, qt_ref, k_ref, vt_ref, ugp_ref = rest
    else:
        qt_ref, k_ref, vt_ref, ugp_ref = rest
    x = x_ref[0]
    shift = mod_ref[0, 0:1, :]
    scale = mod_ref[0, 1:2, :]
    h = (_rmsnorm_rows(x, g_ref[...]) * (1.0 + scale) + shift).astype(BF16)
    z = jnp.dot(h, w_ref[...], preferred_element_type=F32)

    def rotated(lo):
        cols = []
        for c in range(da // LANES):
            zc = z[:, lo + c * LANES: lo + (c + 1) * LANES]
            if rope:
                zc = zc * cos_ref[...] + _swap16(zc) * sin_ref[...]
            cols.append(zc)
        return jnp.concatenate(cols, axis=1)

    head_dim = da // (2 * DA_HEADS)
    q = rotated(0) * (head_dim ** -0.5 * LOG2_E)
    qt_ref[0] = q.T.astype(BF16)
    k_ref[0] = rotated(da).astype(BF16)
    vt_ref[0] = z[:, 2 * da: 3 * da].T.astype(BF16)
    ugp_ref[0] = z[:, 3 * da:]


def _inproj(x, mod, g, w_bf16, rope_tabs, tm):
    b, s, d = x.shape
    in_width = w_bf16.shape[1]
    da = d // 2
    rest = in_width - 3 * da
    per_batch_mod = mod.shape[0] > 1
    rope = rope_tabs is not None
    in_specs = [
        pl.BlockSpec((1, tm, d), lambda bi, i: (bi, i, 0)),
        pl.BlockSpec((1, 6, d), (lambda bi, i: (bi, 0, 0)) if per_batch_mod else (lambda bi, i: (0, 0, 0))),
        _resident((1, d)),
        _resident((d, in_width)),
    ]
    args = [x, mod, g, w_bf16]
    if rope:
        in_specs += [pl.BlockSpec((tm, LANES), lambda bi, i: (i, 0))] * 2
        args += list(rope_tabs)
    return pl.pallas_call(
        functools.partial(_inproj_body, da, rope),
        grid=(b, s // tm),
        in_specs=in_specs,
        out_specs=[
            pl.BlockSpec((1, da, tm), lambda bi, i: (bi, 0, i)),
            pl.BlockSpec((1, tm, da), lambda bi, i: (bi, i, 0)),
            pl.BlockSpec((1, da, tm), lambda bi, i: (bi, 0, i)),
            pl.BlockSpec((1, tm, rest), lambda bi, i: (bi, i, 0)),
        ],
        out_shape=[
            jax.ShapeDtypeStruct((b, da, s), BF16),
            jax.ShapeDtypeStruct((b, s, da), BF16),
            jax.ShapeDtypeStruct((b, da, s), BF16),
            jax.ShapeDtypeStruct((b, s, rest), F32),
        ],
        compiler_params=_params(2),
        name="inproj_rope" if rope else "inproj",
    )(*args)


def _attn_body(n_lat, tk, lam_init, *refs):
    if n_lat:
        (qt_ref, kc_ref, vtc_ref, kl_ref, vtl_ref, lamv_ref, gsub_ref,
         o_ref, vec_ref, vel_ref, m_ref, acc_ref, gap_ref) = refs
    else:
        (qt_ref, kc_ref, vtc_ref, lamv_ref, gsub_ref,
         o_ref, vec_ref, m_ref, acc_ref) = refs
    hd2 = qt_ref.shape[1]
    tq = qt_ref.shape[2]

    @pl.when(pl.program_id(2) == 0)
    def _():
        vec_ref[0:hd2, :] = vtc_ref[0]
        vec_ref[hd2:, :] = jnp.ones((BF16_ROWS, vec_ref.shape[1]), BF16)
        if n_lat:
            for t in range(n_lat):
                vel_ref[t, 0:hd2, :] = vtl_ref[0, :, t * tk:(t + 1) * tk]
                vel_ref[t, hd2:, :] = jnp.ones((BF16_ROWS, tk), BF16)

    qt = qt_ref[0]
    row = lax.broadcasted_iota(jnp.int32, qt.shape, 0)
    zero = jnp.zeros_like(qt)
    q_comp = (jnp.where(row < hd2 // 2, qt, zero), jnp.where(row >= hd2 // 2, qt, zero))

    def first_block(k_blk, ve_blk, j):
        st = jnp.dot(k_blk, q_comp[j], preferred_element_type=F32)
        bm = jnp.max(st, axis=0, keepdims=True)
        pt = jnp.exp2(st - bm).astype(BF16)
        acc_ref[j] = jnp.dot(ve_blk, pt, preferred_element_type=F32)
        m_ref[j, 0:1, :] = bm

    def exact_block(k_blk, ve_blk, j):
        st = jnp.dot(k_blk, q_comp[j], preferred_element_type=F32)
        m_prev = m_ref[j, 0:1, :]
        m_new = jnp.maximum(m_prev, jnp.max(st, axis=0, keepdims=True))
        pt = jnp.exp2(st - m_new).astype(BF16)
        acc_ref[j] = (jnp.exp2(m_prev - m_new) * acc_ref[j]
                      + jnp.dot(ve_blk, pt, preferred_element_type=F32))
        m_ref[j, 0:1, :] = m_new

    def lazy_block(k_blk, ve_blk, j):
        st = jnp.dot(k_blk, q_comp[j], preferred_element_type=F32)
        m_prev = m_ref[j, 0:1, :]
        pt = jnp.exp2(st - m_prev).astype(BF16)
        bm = jnp.max(st, axis=0, keepdims=True)
        m_new = jnp.maximum(m_prev, bm)
        acc_ref[j] = jnp.exp2(m_prev - m_new) * (acc_ref[j] + jnp.dot(ve_blk, pt, preferred_element_type=F32))
        gap_ref[j, 0:1, :] = jnp.maximum(gap_ref[j, 0:1, :], bm - m_prev)
        m_ref[j, 0:1, :] = m_new

    def sweep(block):
        for j in range(2):
            first_block(kc_ref[0], vec_ref[...], j)
        if n_lat:
            def body(t, carry):
                start = pl.multiple_of(t * tk, tk)
                k_blk = kl_ref[0, pl.ds(start, tk), :]
                for j in range(2):
                    block(k_blk, vel_ref[t], j)
                return carry
            lax.fori_loop(0, n_lat, body, 0)

    if n_lat:
        gap_ref[...] = jnp.zeros(gap_ref.shape, F32)
        sweep(lazy_block)
        worst = jnp.max(jnp.maximum(gap_ref[0, 0:1, :], gap_ref[1, 0:1, :]))

        @pl.when(jnp.logical_not(worst <= MAX_LAZY_GAP))
        def _():
            sweep(exact_block)
    else:
        sweep(None)

    lamv = lamv_ref[...]
    lam = (jnp.exp(jnp.sum(lamv[0:1] * lamv[1:2], axis=-1, keepdims=True))
           - jnp.exp(jnp.sum(lamv[2:3] * lamv[3:4], axis=-1, keepdims=True)) + lam_init)
    a0 = acc_ref[0]
    a1 = acc_ref[1]
    o = a0[0:hd2] * (1.0 / a0[hd2:hd2 + 1]) - lam * (a1[0:hd2] * (1.0 / a1[hd2:hd2 + 1]))
    ms = jnp.mean(o * o, axis=0, keepdims=True)
    on = o * lax.rsqrt(ms + EPS) * gsub_ref[...] * (1.0 - lam_init)
    o_ref[0] = on.T.astype(o_ref.dtype)


def _attention(qt, k_c, vt_c, k_l, vt_l, lamv, gsub_col, lam_init, tq, tk):
    b, da, sq = qt.shape
    hd2 = da // DA_HEADS
    sc = k_c.shape[1]
    latent = k_l is not None
    n_lat = k_l.shape[1] // tk if latent else 0
    in_specs = [
        pl.BlockSpec((1, hd2, tq), lambda bi, h, i: (bi, h, i)),
        pl.BlockSpec((1, sc, hd2), lambda bi, h, i: (bi, 0, h)),
        pl.BlockSpec((1, hd2, sc), lambda bi, h, i: (bi, h, 0)),
    ]
    args = [qt, k_c, vt_c]
    scratch = [pltpu.VMEM((hd2 + BF16_ROWS, sc), BF16)]
    if latent:
        sl = k_l.shape[1]
        in_specs += [
            pl.BlockSpec((1, sl, hd2), lambda bi, h, i: (bi, 0, h)),
            pl.BlockSpec((1, hd2, sl), lambda bi, h, i: (bi, h, 0)),
        ]
        args += [k_l, vt_l]
        scratch.append(pltpu.VMEM((n_lat, hd2 + BF16_ROWS, tk), BF16))
    in_specs += [_resident(lamv.shape), _resident(gsub_col.shape)]
    args += [lamv, gsub_col]
    scratch += [pltpu.VMEM((2, SUBLANES, tq), F32), pltpu.VMEM((2, hd2 + BF16_ROWS, tq), F32)]
    if latent:
        scratch.append(pltpu.VMEM((2, SUBLANES, tq), F32))
    return pl.pallas_call(
        functools.partial(_attn_body, n_lat, tk, lam_init),
        grid=(b, DA_HEADS, sq // tq),
        in_specs=in_specs,
        out_specs=pl.BlockSpec((1, tq, hd2), lambda bi, h, i: (bi, i, h)),
        out_shape=jax.ShapeDtypeStruct((b, sq, da), BF16),
        scratch_shapes=scratch,
        compiler_params=_params(3),
        name="attn_latent" if latent else "attn_ctx",
    )(*args)


def _merge_body(seq_len, a_ref, u_ref, gv_ref, p_ref, pprev_ref, pnext_ref, x_ref, mod_ref,
                gvn_ref, ws_ref, bs_ref, gones_ref, wpool_ref, spool_ref, wout_ref, o_ref):
    i = pl.program_id(1)
    tm = u_ref.shape[1]
    cm = u_ref.shape[2]
    gdim = cm // CM_GROUPS
    lane = lax.broadcasted_iota(jnp.int32, (tm, cm), 1)
    grp = lane // gdim

    gv = gv_ref[0]
    gms = jnp.dot(gv * gv, gones_ref[...], preferred_element_type=F32,
                  precision=lax.Precision.HIGHEST)
    vn = (gv * lax.rsqrt(gms + EPS) * gvn_ref[...]).astype(BF16)
    grp_chunk = lax.broadcasted_iota(jnp.int32, (CHUNK, cm), 1) // gdim
    chunks = []
    for c in range(tm // CHUNK):
        vn_c = vn[c * CHUNK:(c + 1) * CHUNK, :]
        vm_c = jnp.zeros((CHUNK, cm), F32)
        for g in range(CM_GROUPS):
            r = jnp.dot(ws_ref[g], vn_c, preferred_element_type=F32)
            vm_c = jnp.where(grp_chunk == g, r, vm_c)
        chunks.append(vm_c + bs_ref[...])
    m_b = u_ref[0] * jnp.concatenate(chunks, axis=0)

    pc = p_ref[0]
    pprev = jnp.where(i == 0, 0.0, pprev_ref[0])
    pnext = jnp.where(i == pl.num_programs(1) - 1, 0.0, pnext_ref[0])
    pe = jnp.concatenate([pprev, pc, pnext], axis=0)

    def sh(o):
        return pe[POOL_HALO + o: POOL_HALO + o + tm, :]

    sums = []
    acc = sh(0)
    half = 0
    for w in POOL_WINDOWS:
        for o in range(-(w // 2), -half):
            acc = acc + sh(o)
        for o in range(max(half, 1), w - w // 2):
            acc = acc + sh(o)
        half = w // 2
        sums.append(acc)
    wsum = sums[-1]
    for g in range(len(POOL_WINDOWS) - 2, -1, -1):
        wsum = jnp.where(grp == g, sums[g], wsum)
    pos = i * tm + lax.broadcasted_iota(jnp.int32, (tm, cm), 0)
    wlo = jnp.full((tm, cm), POOL_WINDOWS[-1] // 2, jnp.int32)
    for g in range(len(POOL_WINDOWS) - 2, -1, -1):
        wlo = jnp.where(grp == g, POOL_WINDOWS[g] // 2, wlo)
    cnt = jnp.minimum(pos + wlo, seq_len) - jnp.maximum(pos - wlo, 0)
    d_pool = (wsum / cnt.astype(F32) - pc).astype(BF16)
    m_c = jnp.dot(d_pool, wpool_ref[...], preferred_element_type=F32) * spool_ref[...]

    mix = jnp.concatenate([a_ref[0], m_b.astype(BF16), m_c.astype(BF16)], axis=1)
    o = jnp.dot(mix, wout_ref[...], preferred_element_type=F32)
    o_ref[0] = x_ref[0] + mod_ref[0, 2:3, :] * o


def _merge(a, ugp, x, mod, gvn, ws_bf16, bs_tab, gones, wpool_bd, spool, wout_bf16, tm):
    b, s, d = x.shape
    da = a.shape[2]
    cm = ugp.shape[2] // 3
    per_batch_mod = mod.shape[0] > 1
    hb = tm // POOL_HALO
    n_hb = s // POOL_HALO
    in_specs = [
        pl.BlockSpec((1, tm, da), lambda bi, i: (bi, i, 0)),
        pl.BlockSpec((1, tm, cm), lambda bi, i: (bi, i, 0)),
        pl.BlockSpec((1, tm, cm), lambda bi, i: (bi, i, 1)),
        pl.BlockSpec((1, tm, cm), lambda bi, i: (bi, i, 2)),
        pl.BlockSpec((1, POOL_HALO, cm), lambda bi, i: (bi, jnp.maximum(i * hb - 1, 0), 2)),
        pl.BlockSpec((1, POOL_HALO, cm), lambda bi, i: (bi, jnp.minimum((i + 1) * hb, n_hb - 1), 2)),
        pl.BlockSpec((1, tm, d), lambda bi, i: (bi, i, 0)),
        pl.BlockSpec((1, 6, d), (lambda bi, i: (bi, 0, 0)) if per_batch_mod else (lambda bi, i: (0, 0, 0))),
        _resident(gvn.shape), _resident(ws_bf16.shape), _resident(bs_tab.shape), _resident(gones.shape),
        _resident(wpool_bd.shape), _resident(spool.shape), _resident(wout_bf16.shape),
    ]
    return pl.pallas_call(
        functools.partial(_merge_body, s),
        grid=(b, s // tm),
        in_specs=in_specs,
        out_specs=pl.BlockSpec((1, tm, d), lambda bi, i: (bi, i, 0)),
        out_shape=jax.ShapeDtypeStruct((b, s, d), F32),
        compiler_params=_params(2),
        name="merge",
    )(a, ugp, ugp, ugp, ugp, ugp, x, mod, gvn, ws_bf16, bs_tab, gones, wpool_bd, spool, wout_bf16)


def _mlp_body(tf, final, x_ref, mod_ref, g_ref, w1_ref, w2_ref, *rest):
    if final:
        gf_ref, o_ref = rest
    else:
        (o_ref,) = rest
    x = x_ref[0]
    shift = mod_ref[0, 3:4, :]
    scale = mod_ref[0, 4:5, :]
    h = (_rmsnorm_rows(x, g_ref[...]) * (1.0 + scale) + shift).astype(BF16)
    acc = jnp.zeros(x.shape, F32)
    for f in range(w1_ref.shape[1] // tf):
        t = jnp.dot(h, w1_ref[:, f * tf:(f + 1) * tf], preferred_element_type=F32)
        t = jnp.square(jnp.maximum(t, 0.0)).astype(BF16)
        acc = acc + jnp.dot(t, w2_ref[f * tf:(f + 1) * tf, :], preferred_element_type=F32)
    y = x + mod_ref[0, 5:6, :] * acc
    if final:
        y = _rmsnorm_rows(y, gf_ref[...])
    o_ref[0] = y


def _mlp(x, mod, g, w1_bf16, w2_bf16, g_final, tm, tf):
    b, s, d = x.shape
    per_batch_mod = mod.shape[0] > 1
    final = g_final is not None
    in_specs = [
        pl.BlockSpec((1, tm, d), lambda bi, i: (bi, i, 0)),
        pl.BlockSpec((1, 6, d), (lambda bi, i: (bi, 0, 0)) if per_batch_mod else (lambda bi, i: (0, 0, 0))),
        _resident(g.shape), _resident(w1_bf16.shape), _resident(w2_bf16.shape),
    ]
    args = [x, mod, g, w1_bf16, w2_bf16]
    if final:
        in_specs.append(_resident(g_final.shape))
        args.append(g_final)
    return pl.pallas_call(
        functools.partial(_mlp_body, tf, final),
        grid=(b, s // tm),
        in_specs=in_specs,
        out_specs=pl.BlockSpec((1, tm, d), lambda bi, i: (bi, i, 0)),
        out_shape=jax.ShapeDtypeStruct((b, s, d), F32),
        compiler_params=_params(2),
        name="mlp_final" if final else "mlp",
    )(*args)


def _rope_tables(n_tokens, head_dim):
    n_rows = n_tokens // GRID_W
    row = jnp.repeat(jnp.arange(n_rows), GRID_W).astype(F32)
    col = jnp.tile(jnp.arange(GRID_W), n_rows).astype(F32)
    n_freq = head_dim // 4
    inv = ROPE_BASE ** (-jnp.arange(n_freq, dtype=F32) / n_freq)
    cr, sr = jnp.cos(row[:, None] * inv), jnp.sin(row[:, None] * inv)
    cc, sc = jnp.cos(col[:, None] * inv), jnp.sin(col[:, None] * inv)
    cos64 = jnp.concatenate([cr, cr, cc, cc], axis=1)
    sin64 = jnp.concatenate([-sr, sr, -sc, sc], axis=1)
    reps = LANES // head_dim
    return jnp.tile(cos64, (1, reps)), jnp.tile(sin64, (1, reps))


def _block_diag(blocks):
    g, r, c = blocks.shape
    out = jnp.zeros((g * r, g * c), blocks.dtype)
    for i in range(g):
        out = out.at[i * r:(i + 1) * r, i * c:(i + 1) * c].set(blocks[i])
    return out


def kernel(x, c, ctx, c_ctx, w_ada, b_ada, g_norm_mix, g_norm_mlp, w_in, lam_q1, lam_k1, lam_q2,
           lam_k2, g_subln, g_vnorm, w_spatial, b_spatial, w_pool, s_pool, w_out, w1, w2, g_final):
    n_layers = w_ada.shape[0]
    batch, seq, d = x.shape
    ctx_len = ctx.shape[1]
    da = d // 2
    head_dim = da // (2 * DA_HEADS)
    cm = d // 4
    gdim = cm // CM_GROUPS

    tm_l, tm_c = min(TM_LATENT, seq), ctx_len
    tq_l, tq_c = min(TQ_LATENT, seq), ctx_len
    tk = min(TK_LATENT, seq)
    tf = TF_MLP

    rows = -(-(batch + 1) // SUBLANES) * SUBLANES
    conds = jnp.zeros((rows, d), F32).at[:batch].set(c).at[batch].set(c_ctx)
    mods = _ada_all_layers(conds, w_ada, b_ada)
    rope_tabs = _rope_tables(seq, head_dim)
    gones = _block_diag(jnp.full((CM_GROUPS, gdim, gdim), 1.0 / gdim, F32))

    xl, xc = x, ctx
    for l in range(n_layers):
        last = l == n_layers - 1
        lam_init = 0.8 - 0.6 * math.exp(-0.3 * l)
        mod_l = mods[l, :batch].reshape(batch, 6, d)
        mod_c = mods[l, batch].reshape(1, 6, d)
        w_in_b = w_in[l].astype(BF16)
        g_mix = g_norm_mix[l][None]
        lamv = jnp.stack([lam_q1[l], lam_k1[l], lam_q2[l], lam_k2[l]]).astype(F32)
        gsub_col = g_subln[l][:, None]
        merge_w = (g_vnorm[l][None], w_spatial[l].astype(BF16),
                   jnp.repeat(b_spatial[l].T, gdim, axis=1), gones,
                   _block_diag(w_pool[l]).astype(BF16), s_pool[l][None], w_out[l].astype(BF16))
        g_mlp = g_norm_mlp[l][None]
        w1_b, w2_b = w1[l].astype(BF16), w2[l].astype(BF16)

        qt_l, k_l, vt_l, ugp_l = _inproj(xl, mod_l, g_mix, w_in_b, rope_tabs, tm_l)
        qt_c, k_c, vt_c, ugp_c = _inproj(xc, mod_c, g_mix, w_in_b, None, tm_c)
        a_l = _attention(qt_l, k_c, vt_c, k_l, vt_l, lamv, gsub_col, lam_init, tq_l, tk)
        xl = _merge(a_l, ugp_l, xl, mod_l, *merge_w, tm_l)
        xl = _mlp(xl, mod_l, g_mlp, w1_b, w2_b, g_final[None] if last else None, tm_l, tf)
        if not last:
            a_c = _attention(qt_c, k_c, vt_c, None, None, lamv, gsub_col, lam_init, tq_c, tk)
            xc = _merge(a_c, ugp_c, xc, mod_c, *merge_w, tm_c)
            xc = _mlp(xc, mod_c, g_mlp, w1_b, w2_b, None, tm_c, tf)
    return xl
```

```python
import functools
import math

import jax
import jax.numpy as jnp
from jax import lax
from jax.experimental import pallas as pl
from jax.experimental.pallas import tpu as pltpu

EPS = 1e-6
GRID_W = 64
ROPE_BASE = 10000.0
DA_HEADS = 4
POOL_WINDOWS = (2, 4, 8, 16)
CM_GROUPS = 4
CHUNK = 128

LANES = 128
SUBLANES = 8
BF16_ROWS = 16
VMEM_LIMIT = 56 * 1024 * 1024
POOL_HALO = 8

TM_LATENT = 512
TQ_LATENT = 1024
TK_LATENT = 512
TF_MLP = 1024

LOG2_E = math.log2(math.e)
MAX_LAZY_GAP = 64.0

F32 = jnp.float32
BF16 = jnp.bfloat16


def _params(n_axes):
    return pltpu.CompilerParams(
        dimension_semantics=("arbitrary",) * n_axes, vmem_limit_bytes=VMEM_LIMIT)


def _resident(shape):
    zeros = (0,) * len(shape)
    return pl.BlockSpec(shape, lambda *_: zeros, pipeline_mode=pl.Buffered(1))


def _rmsnorm_rows(x, g):
    ms = jnp.mean(x * x, axis=-1, keepdims=True)
    return x * lax.rsqrt(ms + EPS) * g


def _ada_body(cond_ref, w_ref, b_ref, o_ref):
    c = cond_ref[...]
    s = c * (1.0 / (1.0 + jnp.exp(-c)))
    o_ref[0] = jnp.dot(s, w_ref[0], preferred_element_type=F32,
                       precision=lax.Precision.HIGHEST) + b_ref[0]


def _ada_all_layers(conds, w_ada, b_ada):
    n_layers, d, n_out = w_ada.shape
    rows = conds.shape[0]
    tn = n_out // 4
    return pl.pallas_call(
        _ada_body,
        grid=(n_layers, n_out // tn),
        in_specs=[
            pl.BlockSpec((rows, d), lambda l, j: (0, 0)),
            pl.BlockSpec((1, d, tn), lambda l, j: (l, 0, j)),
            pl.BlockSpec((1, 1, tn), lambda l, j: (l, 0, j)),
        ],
        out_specs=pl.BlockSpec((1, rows, tn), lambda l, j: (l, 0, j)),
        out_shape=jax.ShapeDtypeStruct((n_layers, rows, n_out), F32),
        compiler_params=_params(2),
        name="ada",
    )(conds, w_ada, b_ada.reshape(n_layers, 1, n_out))


def _swap16(z):
    up = pltpu.roll(z, LANES - 16, axis=1)
    down = pltpu.roll(z, 16, axis=1)
    lane = lax.broadcasted_iota(jnp.int32, z.shape, 1)
    return jnp.where((lane & 16) == 0, up, down)


def _inproj_body(da, rope, x_ref, mod_ref, g_ref, w_ref, *rest):
    if rope:
        cos_ref, sin_ref, qt_ref, k_ref, vt_ref, ugp_ref = rest
    else:
        qt_ref, k_ref, vt_ref, ugp_ref = rest
    x = x_ref[0]
    shift = mod_ref[0, 0:1, :]
    scale = mod_ref[0, 1:2, :]
    h = (_rmsnorm_rows(x, g_ref[...]) * (1.0 + scale) + shift).astype(BF16)
    z = jnp.dot(h, w_ref[...], preferred_element_type=F32)

    def rotated(lo):
        cols = []
        for c in range(da // LANES):
            zc = z[:, lo + c * LANES: lo + (c + 1) * LANES]
            if rope:
                zc = zc * cos_ref[...] + _swap16(zc) * sin_ref[...]
            cols.append(zc)
        return jnp.concatenate(cols, axis=1)

    head_dim = da // (2 * DA_HEADS)
    hd2 = da // DA_HEADS
    q = rotated(0) * (head_dim ** -0.5 * LOG2_E)
    qt_ref[0] = q.T.astype(BF16)
    k = rotated(da).astype(BF16)
    for head in range(DA_HEADS):
        k_ref[0, head] = k[:, head * hd2:(head + 1) * hd2]
    vt_ref[0] = z[:, 2 * da: 3 * da].T.astype(BF16)
    ugp_ref[0] = z[:, 3 * da:]


def _inproj(x, mod, g, w_bf16, rope_tabs, tm):
    b, s, d = x.shape
    in_width = w_bf16.shape[1]
    da = d // 2
    rest = in_width - 3 * da
    per_batch_mod = mod.shape[0] > 1
    rope = rope_tabs is not None
    in_specs = [
        pl.BlockSpec((1, tm, d), lambda bi, i: (bi, i, 0)),
        pl.BlockSpec((1, 6, d), (lambda bi, i: (bi, 0, 0)) if per_batch_mod else (lambda bi, i: (0, 0, 0))),
        _resident((1, d)),
        _resident((d, in_width)),
    ]
    args = [x, mod, g, w_bf16]
    if rope:
        in_specs += [pl.BlockSpec((tm, LANES), lambda bi, i: (i, 0))] * 2
        args += list(rope_tabs)
    return pl.pallas_call(
        functools.partial(_inproj_body, da, rope),
        grid=(b, s // tm),
        in_specs=in_specs,
        out_specs=[
            pl.BlockSpec((1, da, tm), lambda bi, i: (bi, 0, i)),
            pl.BlockSpec((1, DA_HEADS, tm, da // DA_HEADS), lambda bi, i: (bi, 0, i, 0)),
            pl.BlockSpec((1, da, tm), lambda bi, i: (bi, 0, i)),
            pl.BlockSpec((1, tm, rest), lambda bi, i: (bi, i, 0)),
        ],
        out_shape=[
            jax.ShapeDtypeStruct((b, da, s), BF16),
            jax.ShapeDtypeStruct((b, DA_HEADS, s, da // DA_HEADS), BF16),
            jax.ShapeDtypeStruct((b, da, s), BF16),
            jax.ShapeDtypeStruct((b, s, rest), F32),
        ],
        compiler_params=_params(2),
        name="inproj_rope" if rope else "inproj",
    )(*args)


def _attn_body(n_lat, tk, lam_init, *refs):
    if n_lat:
        (qt_ref, kc_ref, vtc_ref, kl_ref, vtl_ref, lamv_ref, gsub_ref,
         o_ref, vec_ref, vel_ref, m_ref, acc_ref, gap_ref) = refs
    else:
        (qt_ref, kc_ref, vtc_ref, lamv_ref, gsub_ref,
         o_ref, vec_ref, m_ref, acc_ref) = refs
    hd2 = qt_ref.shape[1]
    tq = qt_ref.shape[2]

    @pl.when(pl.program_id(2) == 0)
    def _():
        vec_ref[0:hd2, :] = vtc_ref[0]
        vec_ref[hd2:, :] = jnp.ones((BF16_ROWS, vec_ref.shape[1]), BF16)
        if n_lat:
            for t in range(n_lat):
                vel_ref[t, 0:hd2, :] = vtl_ref[0, :, t * tk:(t + 1) * tk]
                vel_ref[t, hd2:, :] = jnp.ones((BF16_ROWS, tk), BF16)

    qt = qt_ref[0]
    row = lax.broadcasted_iota(jnp.int32, qt.shape, 0)
    zero = jnp.zeros_like(qt)
    q_comp = (jnp.where(row < hd2 // 2, qt, zero), jnp.where(row >= hd2 // 2, qt, zero))

    def first_block(k_blk, ve_blk, j):
        st = jnp.dot(k_blk, q_comp[j], preferred_element_type=F32)
        bm = jnp.max(st, axis=0, keepdims=True)
        pt = jnp.exp2(st - bm).astype(BF16)
        acc_ref[j] = jnp.dot(ve_blk, pt, preferred_element_type=F32)
        m_ref[j, 0:1, :] = bm

    def exact_block(k_blk, ve_blk, j):
        st = jnp.dot(k_blk, q_comp[j], preferred_element_type=F32)
        m_prev = m_ref[j, 0:1, :]
        m_new = jnp.maximum(m_prev, jnp.max(st, axis=0, keepdims=True))
        pt = jnp.exp2(st - m_new).astype(BF16)
        acc_ref[j] = (jnp.exp2(m_prev - m_new) * acc_ref[j]
                      + jnp.dot(ve_blk, pt, preferred_element_type=F32))
        m_ref[j, 0:1, :] = m_new

    def lazy_block(k_blk, ve_blk, j):
        st = jnp.dot(k_blk, q_comp[j], preferred_element_type=F32)
        m_prev = m_ref[j, 0:1, :]
        pt = jnp.exp2(st - m_prev).astype(BF16)
        bm = jnp.max(st, axis=0, keepdims=True)
        m_new = jnp.maximum(m_prev, bm)
        acc_ref[j] = jnp.exp2(m_prev - m_new) * (acc_ref[j] + jnp.dot(ve_blk, pt, preferred_element_type=F32))
        gap_ref[j, 0:1, :] = jnp.maximum(gap_ref[j, 0:1, :], bm - m_prev)
        gap_ref[j, 1:2, :] = jnp.maximum(gap_ref[j, 1:2, :], bm)
        m_ref[j, 0:1, :] = m_new

    def sweep(block):
        for j in range(2):
            first_block(kc_ref[0, 0], vec_ref[...], j)
        if n_lat:
            def body(t, carry):
                start = pl.multiple_of(t * tk, tk)
                k_blk = kl_ref[0, 0, pl.ds(start, tk), :]
                for j in range(2):
                    block(k_blk, vel_ref[t], j)
                return carry
            lax.fori_loop(0, n_lat, body, 0)

    if n_lat:
        gap_ref[:, 0:1, :] = jnp.zeros((2, 1, tq), F32)
        gap_ref[:, 1:2, :] = jnp.full((2, 1, tq), -jnp.inf, F32)
        m_ref[...] = jnp.zeros(m_ref.shape, F32)
        acc_ref[...] = jnp.zeros(acc_ref.shape, F32)
        for t in range(n_lat):
            for j in range(2):
                lazy_block(kl_ref[0, 0, t * tk:(t + 1) * tk, :], vel_ref[t], j)
        for j in range(2):
            lazy_block(kc_ref[0, 0], vec_ref[...], j)
        worst_rise = jnp.max(jnp.maximum(gap_ref[0, 0:1, :], gap_ref[1, 0:1, :]))
        lowest_max = jnp.min(jnp.minimum(gap_ref[0, 1:2, :], gap_ref[1, 1:2, :]))
        valid = jnp.logical_and(worst_rise <= MAX_LAZY_GAP, lowest_max >= -MAX_LAZY_GAP)

        @pl.when(jnp.logical_not(valid))
        def _():
            sweep(exact_block)
    else:
        sweep(None)

    lamv = lamv_ref[...]
    lam = (jnp.exp(jnp.sum(lamv[0:1] * lamv[1:2], axis=-1, keepdims=True))
           - jnp.exp(jnp.sum(lamv[2:3] * lamv[3:4], axis=-1, keepdims=True)) + lam_init)
    a0 = acc_ref[0]
    a1 = acc_ref[1]
    o = a0[0:hd2] * (1.0 / a0[hd2:hd2 + 1]) - lam * (a1[0:hd2] * (1.0 / a1[hd2:hd2 + 1]))
    ms = jnp.mean(o * o, axis=0, keepdims=True)
    on = o * lax.rsqrt(ms + EPS) * gsub_ref[...] * (1.0 - lam_init)
    o_ref[0] = on.T.astype(o_ref.dtype)


def _attention(qt, k_c, vt_c, k_l, vt_l, lamv, gsub_col, lam_init, tq, tk):
    b, da, sq = qt.shape
    hd2 = da // DA_HEADS
    sc = k_c.shape[2]
    latent = k_l is not None
    n_lat = k_l.shape[2] // tk if latent else 0
    in_specs = [
        pl.BlockSpec((1, hd2, tq), lambda bi, h, i: (bi, h, i)),
        pl.BlockSpec((1, 1, sc, hd2), lambda bi, h, i: (bi, h, 0, 0)),
        pl.BlockSpec((1, hd2, sc), lambda bi, h, i: (bi, h, 0)),
    ]
    args = [qt, k_c, vt_c]
    scratch = [pltpu.VMEM((hd2 + BF16_ROWS, sc), BF16)]
    if latent:
        sl = k_l.shape[2]
        in_specs += [
            pl.BlockSpec((1, 1, sl, hd2), lambda bi, h, i: (bi, h, 0, 0)),
            pl.BlockSpec((1, hd2, sl), lambda bi, h, i: (bi, h, 0)),
        ]
        args += [k_l, vt_l]
        scratch.append(pltpu.VMEM((n_lat, hd2 + BF16_ROWS, tk), BF16))
    in_specs += [_resident(lamv.shape), _resident(gsub_col.shape)]
    args += [lamv, gsub_col]
    scratch += [pltpu.VMEM((2, SUBLANES, tq), F32), pltpu.VMEM((2, hd2 + BF16_ROWS, tq), F32)]
    if latent:
        scratch.append(pltpu.VMEM((2, SUBLANES, tq), F32))
    return pl.pallas_call(
        functools.partial(_attn_body, n_lat, tk, lam_init),
        grid=(b, DA_HEADS, sq // tq),
        in_specs=in_specs,
        out_specs=pl.BlockSpec((1, tq, hd2), lambda bi, h, i: (bi, i, h)),
        out_shape=jax.ShapeDtypeStruct((b, sq, da), BF16),
        scratch_shapes=scratch,
        compiler_params=_params(3),
        name="attn_latent" if latent else "attn_ctx",
    )(*args)


def _merge_body(seq_len, a_ref, u_ref, gv_ref, p_ref, pprev_ref, pnext_ref, x_ref, mod_ref,
                gvn_ref, ws_ref, bs_ref, gones_ref, wpool_ref, spool_ref, wout_ref, o_ref):
    i = pl.program_id(1)
    tm = u_ref.shape[1]
    cm = u_ref.shape[2]
    da = a_ref.shape[2]
    gdim = cm // CM_GROUPS

    o = jnp.dot(a_ref[0], wout_ref[0:da, :], preferred_element_type=F32)

    gv = gv_ref[0]
    sq = gv * gv
    sq_hi = sq.astype(BF16)
    sq_lo = (sq - sq_hi.astype(F32)).astype(BF16)
    gms = (jnp.dot(sq_hi, gones_ref[...], preferred_element_type=F32)
           + jnp.dot(sq_lo, gones_ref[...], preferred_element_type=F32))
    vn = (gv * lax.rsqrt(gms + EPS) * gvn_ref[...]).astype(BF16)
    grp_chunk = lax.broadcasted_iota(jnp.int32, (CHUNK, cm), 1) // gdim
    chunks = []
    for c in range(tm // CHUNK):
        vn_c = vn[c * CHUNK:(c + 1) * CHUNK, :]
        vm_c = jnp.zeros((CHUNK, cm), F32)
        for g in range(CM_GROUPS):
            r = jnp.dot(ws_ref[g], vn_c, preferred_element_type=F32)
            vm_c = jnp.where(grp_chunk == g, r, vm_c)
        chunks.append(vm_c + bs_ref[...])
    m_b = u_ref[0] * jnp.concatenate(chunks, axis=0)
    o = o + jnp.dot(m_b.astype(BF16), wout_ref[da:da + cm, :], preferred_element_type=F32)

    pc = p_ref[0]
    pprev = jnp.where(i == 0, 0.0, pprev_ref[0])
    pnext = jnp.where(i == pl.num_programs(1) - 1, 0.0, pnext_ref[0])
    pe = jnp.concatenate([pprev, pc, pnext], axis=0)
    pos = i * tm + lax.broadcasted_iota(jnp.int32, (tm, LANES), 0)
    lane = lax.broadcasted_iota(jnp.int32, (tm, LANES), 1)
    means = []
    for col in range(cm // LANES):
        groups = range(col * LANES // gdim, (col + 1) * LANES // gdim)
        trailing = pe[:, col * LANES:(col + 1) * LANES]
        wsum = None
        half = None
        for g in range(groups[-1] + 1):
            w = POOL_WINDOWS[g]
            assert w == 2 ** (g + 1) and w // 2 <= POOL_HALO
            trailing = trailing + pltpu.roll(trailing, w // 2, axis=0)
            if g in groups:
                last_row = POOL_HALO + w - w // 2 - 1
                centred = trailing[last_row:last_row + tm, :]
                in_g = (lane // gdim) == (g - groups[0])
                wsum = centred if wsum is None else jnp.where(in_g, centred, wsum)
                half = jnp.full((tm, LANES), w // 2, jnp.int32) if half is None else jnp.where(in_g, w // 2, half)
        cnt = jnp.minimum(pos + half, seq_len) - jnp.maximum(pos - half, 0)
        means.append(wsum / cnt.astype(F32))
    d_pool = (jnp.concatenate(means, axis=1) - pc).astype(BF16)
    m_c = jnp.dot(d_pool, wpool_ref[...], preferred_element_type=F32) * spool_ref[...]
    o = o + jnp.dot(m_c.astype(BF16), wout_ref[da + cm:, :], preferred_element_type=F32)

    o_ref[0] = x_ref[0] + mod_ref[0, 2:3, :] * o


def _merge(a, ugp, x, mod, gvn, ws_bf16, bs_tab, gones, wpool_bd, spool, wout_bf16, tm):
    b, s, d = x.shape
    da = a.shape[2]
    cm = ugp.shape[2] // 3
    per_batch_mod = mod.shape[0] > 1
    hb = tm // POOL_HALO
    n_hb = s // POOL_HALO
    in_specs = [
        pl.BlockSpec((1, tm, da), lambda bi, i: (bi, i, 0)),
        pl.BlockSpec((1, tm, cm), lambda bi, i: (bi, i, 0)),
        pl.BlockSpec((1, tm, cm), lambda bi, i: (bi, i, 1)),
        pl.BlockSpec((1, tm, cm), lambda bi, i: (bi, i, 2)),
        pl.BlockSpec((1, POOL_HALO, cm), lambda bi, i: (bi, jnp.maximum(i * hb - 1, 0), 2)),
        pl.BlockSpec((1, POOL_HALO, cm), lambda bi, i: (bi, jnp.minimum((i + 1) * hb, n_hb - 1), 2)),
        pl.BlockSpec((1, tm, d), lambda bi, i: (bi, i, 0)),
        pl.BlockSpec((1, 6, d), (lambda bi, i: (bi, 0, 0)) if per_batch_mod else (lambda bi, i: (0, 0, 0))),
        _resident(gvn.shape), _resident(ws_bf16.shape), _resident(bs_tab.shape), _resident(gones.shape),
        _resident(wpool_bd.shape), _resident(spool.shape), _resident(wout_bf16.shape),
    ]
    return pl.pallas_call(
        functools.partial(_merge_body, s),
        grid=(b, s // tm),
        in_specs=in_specs,
        out_specs=pl.BlockSpec((1, tm, d), lambda bi, i: (bi, i, 0)),
        out_shape=jax.ShapeDtypeStruct((b, s, d), F32),
        compiler_params=_params(2),
        name="merge",
    )(a, ugp, ugp, ugp, ugp, ugp, x, mod, gvn, ws_bf16, bs_tab, gones, wpool_bd, spool, wout_bf16)


def _mlp_body(tf, final, x_ref, mod_ref, g_ref, w1_ref, w2_ref, *rest):
    if final:
        gf_ref, o_ref = rest
    else:
        (o_ref,) = rest
    x = x_ref[0]
    shift = mod_ref[0, 3:4, :]
    scale = mod_ref[0, 4:5, :]
    h = (_rmsnorm_rows(x, g_ref[...]) * (1.0 + scale) + shift).astype(BF16)
    acc = jnp.zeros(x.shape, F32)
    for f in range(w1_ref.shape[1] // tf):
        t = jnp.dot(h, w1_ref[:, f * tf:(f + 1) * tf], preferred_element_type=F32)
        t = jnp.square(jnp.maximum(t, 0.0)).astype(BF16)
        acc = acc + jnp.dot(t, w2_ref[f * tf:(f + 1) * tf, :], preferred_element_type=F32)
    y = x + mod_ref[0, 5:6, :] * acc
    if final:
        y = _rmsnorm_rows(y, gf_ref[...])
    o_ref[0] = y


def _mlp(x, mod, g, w1_bf16, w2_bf16, g_final, tm, tf):
    b, s, d = x.shape
    per_batch_mod = mod.shape[0] > 1
    final = g_final is not None
    in_specs = [
        pl.BlockSpec((1, tm, d), lambda bi, i: (bi, i, 0)),
        pl.BlockSpec((1, 6, d), (lambda bi, i: (bi, 0, 0)) if per_batch_mod else (lambda bi, i: (0, 0, 0))),
        _resident(g.shape), _resident(w1_bf16.shape), _resident(w2_bf16.shape),
    ]
    args = [x, mod, g, w1_bf16, w2_bf16]
    if final:
        in_specs.append(_resident(g_final.shape))
        args.append(g_final)
    return pl.pallas_call(
        functools.partial(_mlp_body, tf, final),
        grid=(b, s // tm),
        in_specs=in_specs,
        out_specs=pl.BlockSpec((1, tm, d), lambda bi, i: (bi, i, 0)),
        out_shape=jax.ShapeDtypeStruct((b, s, d), F32),
        compiler_params=_params(2),
        name="mlp_final" if final else "mlp",
    )(*args)


def _rope_tables(n_tokens, head_dim):
    n_rows = n_tokens // GRID_W
    row = jnp.repeat(jnp.arange(n_rows), GRID_W).astype(F32)
    col = jnp.tile(jnp.arange(GRID_W), n_rows).astype(F32)
    n_freq = head_dim // 4
    inv = ROPE_BASE ** (-jnp.arange(n_freq, dtype=F32) / n_freq)
    cr, sr = jnp.cos(row[:, None] * inv), jnp.sin(row[:, None] * inv)
    cc, sc = jnp.cos(col[:, None] * inv), jnp.sin(col[:, None] * inv)
    cos64 = jnp.concatenate([cr, cr, cc, cc], axis=1)
    sin64 = jnp.concatenate([-sr, sr, -sc, sc], axis=1)
    reps = LANES // head_dim
    return jnp.tile(cos64, (1, reps)), jnp.tile(sin64, (1, reps))


def _block_diag(blocks):
    g, r, c = blocks.shape
    out = jnp.zeros((g * r, g * c), blocks.dtype)
    for i in range(g):
        out = out.at[i * r:(i + 1) * r, i * c:(i + 1) * c].set(blocks[i])
    return out


def kernel(x, c, ctx, c_ctx, w_ada, b_ada, g_norm_mix, g_norm_mlp, w_in, lam_q1, lam_k1, lam_q2,
           lam_k2, g_subln, g_vnorm, w_spatial, b_spatial, w_pool, s_pool, w_out, w1, w2, g_final):
    n_layers = w_ada.shape[0]
    batch, seq, d = x.shape
    ctx_len = ctx.shape[1]
    da = d // 2
    head_dim = da // (2 * DA_HEADS)
    cm = d // 4
    gdim = cm // CM_GROUPS

    tm_l, tm_c = min(TM_LATENT, seq), ctx_len
    tq_l, tq_c = min(TQ_LATENT, seq), ctx_len
    tk = min(TK_LATENT, seq)
    tf = TF_MLP

    rows = -(-(batch + 1) // SUBLANES) * SUBLANES
    conds = jnp.zeros((rows, d), F32).at[:batch].set(c).at[batch].set(c_ctx)
    mods = _ada_all_layers(conds, w_ada, b_ada)
    rope_tabs = _rope_tables(seq, head_dim)
    gones = _block_diag(jnp.full((CM_GROUPS, gdim, gdim), 1.0 / gdim, F32)).astype(BF16)

    xl, xc = x, ctx
    for l in range(n_layers):
        last = l == n_layers - 1
        lam_init = 0.8 - 0.6 * math.exp(-0.3 * l)
        mod_l = mods[l, :batch].reshape(batch, 6, d)
        mod_c = mods[l, batch].reshape(1, 6, d)
        w_in_b = w_in[l].astype(BF16)
        g_mix = g_norm_mix[l][None]
        lamv = jnp.stack([lam_q1[l], lam_k1[l], lam_q2[l], lam_k2[l]]).astype(F32)
        gsub_col = g_subln[l][:, None]
        merge_w = (g_vnorm[l][None], w_spatial[l].astype(BF16),
                   jnp.repeat(b_spatial[l].T, gdim, axis=1), gones,
                   _block_diag(w_pool[l]).astype(BF16), s_pool[l][None], w_out[l].astype(BF16))
        g_mlp = g_norm_mlp[l][None]
        w1_b, w2_b = w1[l].astype(BF16), w2[l].astype(BF16)

        qt_l, k_l, vt_l, ugp_l = _inproj(xl, mod_l, g_mix, w_in_b, rope_tabs, tm_l)
        qt_c, k_c, vt_c, ugp_c = _inproj(xc, mod_c, g_mix, w_in_b, None, tm_c)
        a_l = _attention(qt_l, k_c, vt_c, k_l, vt_l, lamv, gsub_col, lam_init, tq_l, tk)
        xl = _merge(a_l, ugp_l, xl, mod_l, *merge_w, tm_l)
        xl = _mlp(xl, mod_l, g_mlp, w1_b, w2_b, g_final[None] if last else None, tm_l, tf)
        if not last:
            a_c = _attention(qt_c, k_c, vt_c, None, None, lamv, gsub_col, lam_init, tq_c, tk)
            xc = _merge(a_c, ugp_c, xc, mod_c, *merge_w, tm_c)
            xc = _mlp(xc, mod_c, g_mlp, w1_b, w2_b, None, tm_c, tf)
    return xl
```

```python
import functools
import math

import jax
import jax.numpy as jnp
from jax import lax
from jax.experimental import pallas as pl
from jax.experimental.pallas import tpu as pltpu

EPS = 1e-6
GRID_W = 64
ROPE_BASE = 10000.0
DA_HEADS = 4
POOL_WINDOWS = (2, 4, 8, 16)
CM_GROUPS = 4
CHUNK = 128

LANES = 128
SUBLANES = 8
BF16_ROWS = 16
VMEM_LIMIT = 56 * 1024 * 1024
POOL_HALO = 8

TM_LATENT = 1024
TQ_LATENT = 2048
TK_LATENT = 512
TF_MLP = 1024

LOG2_E = math.log2(math.e)
MAX_LAZY_GAP = 64.0

F32 = jnp.float32
BF16 = jnp.bfloat16


def _params(n_axes):
    return pltpu.CompilerParams(
        dimension_semantics=("arbitrary",) * n_axes, vmem_limit_bytes=VMEM_LIMIT)


def _resident(shape):
    zeros = (0,) * len(shape)
    return pl.BlockSpec(shape, lambda *_: zeros, pipeline_mode=pl.Buffered(1))


def _rmsnorm_rows(x, g):
    ms = jnp.mean(x * x, axis=-1, keepdims=True)
    return x * lax.rsqrt(ms + EPS) * g


def _ada_body(cond_ref, w_ref, b_ref, o_ref):
    c = cond_ref[...]
    s = c * (1.0 / (1.0 + jnp.exp(-c)))
    o_ref[0] = jnp.dot(s, w_ref[0], preferred_element_type=F32,
                       precision=lax.Precision.HIGHEST) + b_ref[0]


def _ada_all_layers(conds, w_ada, b_ada):
    n_layers, d, n_out = w_ada.shape
    rows = conds.shape[0]
    tn = n_out // 4
    return pl.pallas_call(
        _ada_body,
        grid=(n_layers, n_out // tn),
        in_specs=[
            pl.BlockSpec((rows, d), lambda l, j: (0, 0)),
            pl.BlockSpec((1, d, tn), lambda l, j: (l, 0, j)),
            pl.BlockSpec((1, 1, tn), lambda l, j: (l, 0, j)),
        ],
        out_specs=pl.BlockSpec((1, rows, tn), lambda l, j: (l, 0, j)),
        out_shape=jax.ShapeDtypeStruct((n_layers, rows, n_out), F32),
        compiler_params=_params(2),
        name="ada",
    )(conds, w_ada, b_ada.reshape(n_layers, 1, n_out))


def _swap16(z):
    up = pltpu.roll(z, LANES - 16, axis=1)
    down = pltpu.roll(z, 16, axis=1)
    lane = lax.broadcasted_iota(jnp.int32, z.shape, 1)
    return jnp.where((lane & 16) == 0, up, down)


def _inproj_body(da, rope, x_ref, mod_ref, g_ref, w_ref, *rest):
    if rope:
        cos_ref, sin_ref, qt_ref, k_ref, vt_ref, ugp_ref = rest
    else:
        qt_ref, k_ref, vt_ref, ugp_ref = rest
    x = x_ref[0]
    shift = mod_ref[0, 0:1, :]
    scale = mod_ref[0, 1:2, :]
    h = (_rmsnorm_rows(x, g_ref[...]) * (1.0 + scale) + shift).astype(BF16)
    z = jnp.dot(h, w_ref[...], preferred_element_type=F32)

    def rotated(lo):
        cols = []
        for c in range(da // LANES):
            zc = z[:, lo + c * LANES: lo + (c + 1) * LANES]
            if rope:
                zc = zc * cos_ref[...] + _swap16(zc) * sin_ref[...]
            cols.append(zc)
        return jnp.concatenate(cols, axis=1)

    head_dim = da // (2 * DA_HEADS)
    hd2 = da // DA_HEADS
    q = rotated(0) * (head_dim ** -0.5 * LOG2_E)
    qt_ref[0] = q.T.astype(BF16)
    k = rotated(da).astype(BF16)
    for head in range(DA_HEADS):
        k_ref[0, head] = k[:, head * hd2:(head + 1) * hd2]
    vt_ref[0] = z[:, 2 * da: 3 * da].T.astype(BF16)
    ugp_ref[0] = z[:, 3 * da:]


def _inproj(x, mod, g, w_bf16, rope_tabs, tm):
    b, s, d = x.shape
    in_width = w_bf16.shape[1]
    da = d // 2
    rest = in_width - 3 * da
    per_batch_mod = mod.shape[0] > 1
    rope = rope_tabs is not None
    in_specs = [
        pl.BlockSpec((1, tm, d), lambda bi, i: (bi, i, 0)),
        pl.BlockSpec((1, 6, d), (lambda bi, i: (bi, 0, 0)) if per_batch_mod else (lambda bi, i: (0, 0, 0))),
        _resident((1, d)),
        _resident((d, in_width)),
    ]
    args = [x, mod, g, w_bf16]
    if rope:
        in_specs += [pl.BlockSpec((tm, LANES), lambda bi, i: (i, 0))] * 2
        args += list(rope_tabs)
    return pl.pallas_call(
        functools.partial(_inproj_body, da, rope),
        grid=(b, s // tm),
        in_specs=in_specs,
        out_specs=[
            pl.BlockSpec((1, da, tm), lambda bi, i: (bi, 0, i)),
            pl.BlockSpec((1, DA_HEADS, tm, da // DA_HEADS), lambda bi, i: (bi, 0, i, 0)),
            pl.BlockSpec((1, da, tm), lambda bi, i: (bi, 0, i)),
            pl.BlockSpec((1, tm, rest), lambda bi, i: (bi, i, 0)),
        ],
        out_shape=[
            jax.ShapeDtypeStruct((b, da, s), BF16),
            jax.ShapeDtypeStruct((b, DA_HEADS, s, da // DA_HEADS), BF16),
            jax.ShapeDtypeStruct((b, da, s), BF16),
            jax.ShapeDtypeStruct((b, s, rest), F32),
        ],
        compiler_params=_params(2),
        name="inproj_rope" if rope else "inproj",
    )(*args)


def _attn_body(n_lat, tk, lam_init, *refs):
    if n_lat:
        (qt_ref, kc_ref, vtc_ref, kl_ref, vtl_ref, lamv_ref, gsub_ref,
         o_ref, vec_ref, vel_ref, m_ref, acc_ref, gap_ref) = refs
    else:
        (qt_ref, kc_ref, vtc_ref, lamv_ref, gsub_ref,
         o_ref, vec_ref, m_ref, acc_ref) = refs
    hd2 = qt_ref.shape[1]
    tq = qt_ref.shape[2]

    @pl.when(pl.program_id(2) == 0)
    def _():
        vec_ref[0:hd2, :] = vtc_ref[0]
        vec_ref[hd2:, :] = jnp.ones((BF16_ROWS, vec_ref.shape[1]), BF16)
        if n_lat:
            for t in range(n_lat):
                vel_ref[t, 0:hd2, :] = vtl_ref[0, :, t * tk:(t + 1) * tk]
                vel_ref[t, hd2:, :] = jnp.ones((BF16_ROWS, tk), BF16)

    qt = qt_ref[0]
    row = lax.broadcasted_iota(jnp.int32, qt.shape, 0)
    zero = jnp.zeros_like(qt)
    q_comp = (jnp.where(row < hd2 // 2, qt, zero), jnp.where(row >= hd2 // 2, qt, zero))

    def first_block(k_blk, ve_blk, j):
        st = jnp.dot(k_blk, q_comp[j], preferred_element_type=F32)
        bm = jnp.max(st, axis=0, keepdims=True)
        pt = jnp.exp2(st - bm).astype(BF16)
        acc_ref[j] = jnp.dot(ve_blk, pt, preferred_element_type=F32)
        m_ref[j, 0:1, :] = bm

    def exact_block(k_blk, ve_blk, j):
        st = jnp.dot(k_blk, q_comp[j], preferred_element_type=F32)
        m_prev = m_ref[j, 0:1, :]
        m_new = jnp.maximum(m_prev, jnp.max(st, axis=0, keepdims=True))
        pt = jnp.exp2(st - m_new).astype(BF16)
        acc_ref[j] = (jnp.exp2(m_prev - m_new) * acc_ref[j]
                      + jnp.dot(ve_blk, pt, preferred_element_type=F32))
        m_ref[j, 0:1, :] = m_new

    def lazy_block(k_blk, ve_blk, j):
        st = jnp.dot(k_blk, q_comp[j], preferred_element_type=F32)
        m_prev = m_ref[j, 0:1, :]
        pt = jnp.exp2(st - m_prev).astype(BF16)
        bm = jnp.max(st, axis=0, keepdims=True)
        m_new = jnp.maximum(m_prev, bm)
        acc_ref[j] = jnp.exp2(m_prev - m_new) * (acc_ref[j] + jnp.dot(ve_blk, pt, preferred_element_type=F32))
        gap_ref[j, 0:1, :] = jnp.maximum(gap_ref[j, 0:1, :], bm - m_prev)
        gap_ref[j, 1:2, :] = jnp.maximum(gap_ref[j, 1:2, :], bm)
        m_ref[j, 0:1, :] = m_new

    def finalize():
        lamv = lamv_ref[...]
        lam = (jnp.exp(jnp.sum(lamv[0:1] * lamv[1:2], axis=-1, keepdims=True))
               - jnp.exp(jnp.sum(lamv[2:3] * lamv[3:4], axis=-1, keepdims=True)) + lam_init)
        a0 = acc_ref[0]
        a1 = acc_ref[1]
        o = a0[0:hd2] * (1.0 / a0[hd2:hd2 + 1]) - lam * (a1[0:hd2] * (1.0 / a1[hd2:hd2 + 1]))
        ms = jnp.mean(o * o, axis=0, keepdims=True)
        on = o * lax.rsqrt(ms + EPS) * gsub_ref[...] * (1.0 - lam_init)
        o_ref[0] = on.T.astype(o_ref.dtype)

    def sweep(block):
        for j in range(2):
            first_block(kc_ref[0, 0], vec_ref[...], j)
        if n_lat:
            def body(t, carry):
                start = pl.multiple_of(t * tk, tk)
                k_blk = kl_ref[0, 0, pl.ds(start, tk), :]
                for j in range(2):
                    block(k_blk, vel_ref[t], j)
                return carry
            lax.fori_loop(0, n_lat, body, 0)

    if n_lat:
        gap_ref[:, 0:1, :] = jnp.zeros((2, 1, tq), F32)
        gap_ref[:, 1:2, :] = jnp.full((2, 1, tq), -jnp.inf, F32)
        m_ref[...] = jnp.zeros(m_ref.shape, F32)
        acc_ref[...] = jnp.zeros(acc_ref.shape, F32)
        for t in range(n_lat):
            for j in range(2):
                lazy_block(kl_ref[0, 0, t * tk:(t + 1) * tk, :], vel_ref[t], j)
            if t == (n_lat - 1) // 2:
                for j in range(2):
                    lazy_block(kc_ref[0, 0], vec_ref[...], j)
        worst_rise = jnp.max(jnp.maximum(gap_ref[0, 0:1, :], gap_ref[1, 0:1, :]))
        lowest_max = jnp.min(jnp.minimum(gap_ref[0, 1:2, :], gap_ref[1, 1:2, :]))
        valid = jnp.logical_and(worst_rise <= MAX_LAZY_GAP, lowest_max >= -MAX_LAZY_GAP)
        finalize()

        @pl.when(jnp.logical_not(valid))
        def _():
            sweep(exact_block)
            finalize()
    else:
        sweep(None)
        finalize()


def _attention(qt, k_c, vt_c, k_l, vt_l, lamv, gsub_col, lam_init, tq, tk):
    b, da, sq = qt.shape
    hd2 = da // DA_HEADS
    sc = k_c.shape[2]
    latent = k_l is not None
    n_lat = k_l.shape[2] // tk if latent else 0
    in_specs = [
        pl.BlockSpec((1, hd2, tq), lambda bi, h, i: (bi, h, i)),
        pl.BlockSpec((1, 1, sc, hd2), lambda bi, h, i: (bi, h, 0, 0)),
        pl.BlockSpec((1, hd2, sc), lambda bi, h, i: (bi, h, 0)),
    ]
    args = [qt, k_c, vt_c]
    scratch = [pltpu.VMEM((hd2 + BF16_ROWS, sc), BF16)]
    if latent:
        sl = k_l.shape[2]
        in_specs += [
            pl.BlockSpec((1, 1, sl, hd2), lambda bi, h, i: (bi, h, 0, 0)),
            pl.BlockSpec((1, hd2, sl), lambda bi, h, i: (bi, h, 0)),
        ]
        args += [k_l, vt_l]
        scratch.append(pltpu.VMEM((n_lat, hd2 + BF16_ROWS, tk), BF16))
    in_specs += [_resident(lamv.shape), _resident(gsub_col.shape)]
    args += [lamv, gsub_col]
    scratch += [pltpu.VMEM((2, SUBLANES, tq), F32), pltpu.VMEM((2, hd2 + BF16_ROWS, tq), F32)]
    if latent:
        scratch.append(pltpu.VMEM((2, SUBLANES, tq), F32))
    return pl.pallas_call(
        functools.partial(_attn_body, n_lat, tk, lam_init),
        grid=(b, DA_HEADS, sq // tq),
        in_specs=in_specs,
        out_specs=pl.BlockSpec((1, tq, hd2), lambda bi, h, i: (bi, i, h)),
        out_shape=jax.ShapeDtypeStruct((b, sq, da), BF16),
        scratch_shapes=scratch,
        compiler_params=_params(3),
        name="attn_latent" if latent else "attn_ctx",
    )(*args)


def _merge_body(seq_len, a_ref, u_ref, gv_ref, p_ref, pprev_ref, pnext_ref, x_ref, mod_ref,
                gvn_ref, ws_ref, bs_ref, gones_ref, wpool_ref, spool_ref, wout_ref, o_ref):
    i = pl.program_id(1)
    tm = u_ref.shape[1]
    cm = u_ref.shape[2]
    da = a_ref.shape[2]
    gdim = cm // CM_GROUPS

    o = jnp.dot(a_ref[0], wout_ref[0:da, :], preferred_element_type=F32)

    gv = gv_ref[0]
    sq = gv * gv
    sq_hi = sq.astype(BF16)
    sq_lo = (sq - sq_hi.astype(F32)).astype(BF16)
    gms = (jnp.dot(sq_hi, gones_ref[...], preferred_element_type=F32)
           + jnp.dot(sq_lo, gones_ref[...], preferred_element_type=F32))
    vn = gv * lax.rsqrt(gms + EPS) * gvn_ref[...]
    grp_chunk = lax.broadcasted_iota(jnp.int32, (CHUNK, cm), 1) // gdim
    chunks = []
    for c in range(tm // CHUNK):
        vn_c = vn[c * CHUNK:(c + 1) * CHUNK, :]
        stacked = jnp.concatenate(
            [jnp.where(grp_chunk == g, vn_c, 0.0).astype(BF16) for g in range(CM_GROUPS)], axis=0)
        chunks.append(jnp.dot(ws_ref[...], stacked, preferred_element_type=F32) + bs_ref[...])
    m_b = u_ref[0] * jnp.concatenate(chunks, axis=0)
    o = o + jnp.dot(m_b.astype(BF16), wout_ref[da:da + cm, :], preferred_element_type=F32)

    pc = p_ref[0]
    pprev = jnp.where(i == 0, 0.0, pprev_ref[0])
    pnext = jnp.where(i == pl.num_programs(1) - 1, 0.0, pnext_ref[0])
    pe = jnp.concatenate([pprev, pc, pnext], axis=0)
    pos = i * tm + lax.broadcasted_iota(jnp.int32, (tm, LANES), 0)
    lane = lax.broadcasted_iota(jnp.int32, (tm, LANES), 1)
    means = []
    for col in range(cm // LANES):
        groups = range(col * LANES // gdim, (col + 1) * LANES // gdim)
        trailing = pe[:, col * LANES:(col + 1) * LANES]
        wsum = None
        half = None
        for g in range(groups[-1] + 1):
            w = POOL_WINDOWS[g]
            assert w == 2 ** (g + 1) and w // 2 <= POOL_HALO
            trailing = trailing + pltpu.roll(trailing, w // 2, axis=0)
            if g in groups:
                last_row = POOL_HALO + w - w // 2 - 1
                centred = trailing[last_row:last_row + tm, :]
                in_g = (lane // gdim) == (g - groups[0])
                wsum = centred if wsum is None else jnp.where(in_g, centred, wsum)
                half = jnp.full((tm, LANES), w // 2, jnp.int32) if half is None else jnp.where(in_g, w // 2, half)
        cnt = jnp.minimum(pos + half, seq_len) - jnp.maximum(pos - half, 0)
        means.append(wsum / cnt.astype(F32))
    d_pool = (jnp.concatenate(means, axis=1) - pc).astype(BF16)
    m_c = jnp.dot(d_pool, wpool_ref[...], preferred_element_type=F32) * spool_ref[...]
    o = o + jnp.dot(m_c.astype(BF16), wout_ref[da + cm:, :], preferred_element_type=F32)

    o_ref[0] = x_ref[0] + mod_ref[0, 2:3, :] * o


def _merge(a, ugp, x, mod, gvn, ws_bf16, bs_tab, gones, wpool_bd, spool, wout_bf16, tm):
    b, s, d = x.shape
    da = a.shape[2]
    cm = ugp.shape[2] // 3
    per_batch_mod = mod.shape[0] > 1
    hb = tm // POOL_HALO
    n_hb = s // POOL_HALO
    in_specs = [
        pl.BlockSpec((1, tm, da), lambda bi, i: (bi, i, 0)),
        pl.BlockSpec((1, tm, cm), lambda bi, i: (bi, i, 0)),
        pl.BlockSpec((1, tm, cm), lambda bi, i: (bi, i, 1)),
        pl.BlockSpec((1, tm, cm), lambda bi, i: (bi, i, 2)),
        pl.BlockSpec((1, POOL_HALO, cm), lambda bi, i: (bi, jnp.maximum(i * hb - 1, 0), 2)),
        pl.BlockSpec((1, POOL_HALO, cm), lambda bi, i: (bi, jnp.minimum((i + 1) * hb, n_hb - 1), 2)),
        pl.BlockSpec((1, tm, d), lambda bi, i: (bi, i, 0)),
        pl.BlockSpec((1, 6, d), (lambda bi, i: (bi, 0, 0)) if per_batch_mod else (lambda bi, i: (0, 0, 0))),
        _resident(gvn.shape), _resident(ws_bf16.shape), _resident(bs_tab.shape), _resident(gones.shape),
        _resident(wpool_bd.shape), _resident(spool.shape), _resident(wout_bf16.shape),
    ]
    return pl.pallas_call(
        functools.partial(_merge_body, s),
        grid=(b, s // tm),
        in_specs=in_specs,
        out_specs=pl.BlockSpec((1, tm, d), lambda bi, i: (bi, i, 0)),
        out_shape=jax.ShapeDtypeStruct((b, s, d), F32),
        compiler_params=_params(2),
        name="merge",
    )(a, ugp, ugp, ugp, ugp, ugp, x, mod, gvn, ws_bf16, bs_tab, gones, wpool_bd, spool, wout_bf16)


def _mlp_body(tf, final, x_ref, mod_ref, g_ref, w1_ref, w2_ref, *rest):
    if final:
        gf_ref, o_ref = rest
    else:
        (o_ref,) = rest
    x = x_ref[0]
    shift = mod_ref[0, 3:4, :]
    scale = mod_ref[0, 4:5, :]
    h = (_rmsnorm_rows(x, g_ref[...]) * (1.0 + scale) + shift).astype(BF16)
    acc = jnp.zeros(x.shape, F32)
    for f in range(w1_ref.shape[1] // tf):
        t = jnp.dot(h, w1_ref[:, f * tf:(f + 1) * tf], preferred_element_type=F32)
        t = jnp.square(jnp.maximum(t, 0.0)).astype(BF16)
        acc = acc + jnp.dot(t, w2_ref[f * tf:(f + 1) * tf, :], preferred_element_type=F32)
    y = x + mod_ref[0, 5:6, :] * acc
    if final:
        y = _rmsnorm_rows(y, gf_ref[...])
    o_ref[0] = y


def _mlp(x, mod, g, w1_bf16, w2_bf16, g_final, tm, tf):
    b, s, d = x.shape
    per_batch_mod = mod.shape[0] > 1
    final = g_final is not None
    in_specs = [
        pl.BlockSpec((1, tm, d), lambda bi, i: (bi, i, 0)),
        pl.BlockSpec((1, 6, d), (lambda bi, i: (bi, 0, 0)) if per_batch_mod else (lambda bi, i: (0, 0, 0))),
        _resident(g.shape), _resident(w1_bf16.shape), _resident(w2_bf16.shape),
    ]
    args = [x, mod, g, w1_bf16, w2_bf16]
    if final:
        in_specs.append(_resident(g_final.shape))
        args.append(g_final)
    return pl.pallas_call(
        functools.partial(_mlp_body, tf, final),
        grid=(b, s // tm),
        in_specs=in_specs,
        out_specs=pl.BlockSpec((1, tm, d), lambda bi, i: (bi, i, 0)),
        out_shape=jax.ShapeDtypeStruct((b, s, d), F32),
        compiler_params=_params(2),
        name="mlp_final" if final else "mlp",
    )(*args)


def _rope_tables(n_tokens, head_dim):
    n_rows = n_tokens // GRID_W
    row = jnp.repeat(jnp.arange(n_rows), GRID_W).astype(F32)
    col = jnp.tile(jnp.arange(GRID_W), n_rows).astype(F32)
    n_freq = head_dim // 4
    inv = ROPE_BASE ** (-jnp.arange(n_freq, dtype=F32) / n_freq)
    cr, sr = jnp.cos(row[:, None] * inv), jnp.sin(row[:, None] * inv)
    cc, sc = jnp.cos(col[:, None] * inv), jnp.sin(col[:, None] * inv)
    cos64 = jnp.concatenate([cr, cr, cc, cc], axis=1)
    sin64 = jnp.concatenate([-sr, sr, -sc, sc], axis=1)
    reps = LANES // head_dim
    return jnp.tile(cos64, (1, reps)), jnp.tile(sin64, (1, reps))


def _block_diag(blocks):
    g, r, c = blocks.shape
    out = jnp.zeros((g * r, g * c), blocks.dtype)
    for i in range(g):
        out = out.at[i * r:(i + 1) * r, i * c:(i + 1) * c].set(blocks[i])
    return out


def kernel(x, c, ctx, c_ctx, w_ada, b_ada, g_norm_mix, g_norm_mlp, w_in, lam_q1, lam_k1, lam_q2,
           lam_k2, g_subln, g_vnorm, w_spatial, b_spatial, w_pool, s_pool, w_out, w1, w2, g_final):
    n_layers = w_ada.shape[0]
    batch, seq, d = x.shape
    ctx_len = ctx.shape[1]
    da = d // 2
    head_dim = da // (2 * DA_HEADS)
    cm = d // 4
    gdim = cm // CM_GROUPS

    tm_l, tm_c = min(TM_LATENT, seq), ctx_len
    tq_l, tq_c = min(TQ_LATENT, seq), ctx_len
    tk = min(TK_LATENT, seq)
    tf = TF_MLP

    rows = -(-(batch + 1) // SUBLANES) * SUBLANES
    conds = jnp.zeros((rows, d), F32).at[:batch].set(c).at[batch].set(c_ctx)
    mods = _ada_all_layers(conds, w_ada, b_ada)
    rope_tabs = _rope_tables(seq, head_dim)
    gones = _block_diag(jnp.full((CM_GROUPS, gdim, gdim), 1.0 / gdim, F32)).astype(BF16)

    xl, xc = x, ctx
    for l in range(n_layers):
        last = l == n_layers - 1
        lam_init = 0.8 - 0.6 * math.exp(-0.3 * l)
        mod_l = mods[l, :batch].reshape(batch, 6, d)
        mod_c = mods[l, batch].reshape(1, 6, d)
        w_in_b = w_in[l].astype(BF16)
        g_mix = g_norm_mix[l][None]
        lamv = jnp.stack([lam_q1[l], lam_k1[l], lam_q2[l], lam_k2[l]]).astype(F32)
        gsub_col = g_subln[l][:, None]
        merge_w = (g_vnorm[l][None],
                   jnp.transpose(w_spatial[l], (1, 0, 2)).reshape(CHUNK, CM_GROUPS * CHUNK).astype(BF16),
                   jnp.repeat(b_spatial[l].T, gdim, axis=1), gones,
                   _block_diag(w_pool[l]).astype(BF16), s_pool[l][None], w_out[l].astype(BF16))
        g_mlp = g_norm_mlp[l][None]
        w1_b, w2_b = w1[l].astype(BF16), w2[l].astype(BF16)

        qt_l, k_l, vt_l, ugp_l = _inproj(xl, mod_l, g_mix, w_in_b, rope_tabs, tm_l)
        qt_c, k_c, vt_c, ugp_c = _inproj(xc, mod_c, g_mix, w_in_b, None, tm_c)
        a_l = _attention(qt_l, k_c, vt_c, k_l, vt_l, lamv, gsub_col, lam_init, tq_l, tk)
        xl = _merge(a_l, ugp_l, xl, mod_l, *merge_w, tm_l)
        xl = _mlp(xl, mod_l, g_mlp, w1_b, w2_b, g_final[None] if last else None, tm_l, tf)
        if not last:
            a_c = _attention(qt_c, k_c, vt_c, None, None, lamv, gsub_col, lam_init, tq_c, tk)
            xc = _merge(a_c, ugp_c, xc, mod_c, *merge_w, tm_c)
            xc = _mlp(xc, mod_c, g_mlp, w1_b, w2_b, None, tm_c, tf)
    return xl
```

```python
import functools
import math

import jax
import jax.numpy as jnp
from jax import lax
from jax.experimental import pallas as pl
from jax.experimental.pallas import tpu as pltpu

EPS = 1e-6
GRID_W = 64
ROPE_BASE = 10000.0
DA_HEADS = 4
POOL_WINDOWS = (2, 4, 8, 16)
CM_GROUPS = 4
CHUNK = 128

LANES = 128
SUBLANES = 8
BF16_ROWS = 16
VMEM_LIMIT = 56 * 1024 * 1024
POOL_HALO = 8

TM_LATENT = 1024
TQ_LATENT = 2048
TK_LATENT = 512
TF_MLP = 1024

LOG2_E = math.log2(math.e)
MAX_LAZY_GAP = 64.0

F32 = jnp.float32
BF16 = jnp.bfloat16


def _params(n_axes):
    return pltpu.CompilerParams(
        dimension_semantics=("arbitrary",) * n_axes, vmem_limit_bytes=VMEM_LIMIT)


def _resident(shape):
    zeros = (0,) * len(shape)
    return pl.BlockSpec(shape, lambda *_: zeros, pipeline_mode=pl.Buffered(1))


def _resident_layer(stack_shape, layer):
    zeros = (0,) * (len(stack_shape) - 1)
    return pl.BlockSpec((None,) + tuple(stack_shape[1:]), lambda *_: (layer,) + zeros,
                        pipeline_mode=pl.Buffered(1))


def _rmsnorm_rows(x, g):
    ms = jnp.mean(x * x, axis=-1, keepdims=True)
    return x * lax.rsqrt(ms + EPS) * g


def _ada_body(cond_ref, w_ref, b_ref, o_ref):
    c = cond_ref[...]
    s = c * (1.0 / (1.0 + jnp.exp(-c)))
    o_ref[0] = jnp.dot(s, w_ref[0], preferred_element_type=F32,
                       precision=lax.Precision.HIGHEST) + b_ref[0]


def _ada_all_layers(conds, w_ada, b_ada):
    n_layers, d, n_out = w_ada.shape
    rows = conds.shape[0]
    tn = n_out // 4
    return pl.pallas_call(
        _ada_body,
        grid=(n_layers, n_out // tn),
        in_specs=[
            pl.BlockSpec((rows, d), lambda l, j: (0, 0)),
            pl.BlockSpec((1, d, tn), lambda l, j: (l, 0, j)),
            pl.BlockSpec((1, 1, tn), lambda l, j: (l, 0, j)),
        ],
        out_specs=pl.BlockSpec((1, rows, tn), lambda l, j: (l, 0, j)),
        out_shape=jax.ShapeDtypeStruct((n_layers, rows, n_out), F32),
        compiler_params=_params(2),
        name="ada",
    )(conds, w_ada, b_ada.reshape(n_layers, 1, n_out))


def _swap16(z):
    up = pltpu.roll(z, LANES - 16, axis=1)
    down = pltpu.roll(z, 16, axis=1)
    lane = lax.broadcasted_iota(jnp.int32, z.shape, 1)
    return jnp.where((lane & 16) == 0, up, down)


def _inproj_body(da, rope, x_ref, mod_ref, g_ref, w_ref, *rest):
    if rope:
        cos_ref, sin_ref, qt_ref, k_ref, vt_ref, ugp_ref = rest
    else:
        qt_ref, k_ref, vt_ref, ugp_ref = rest
    x = x_ref[0]
    shift = mod_ref[0, 0:1, :]
    scale = mod_ref[0, 1:2, :]
    h = (_rmsnorm_rows(x, g_ref[...]) * (1.0 + scale) + shift).astype(BF16)
    z = jnp.dot(h, w_ref[...], preferred_element_type=F32)

    def rotated(lo):
        cols = []
        for c in range(da // LANES):
            zc = z[:, lo + c * LANES: lo + (c + 1) * LANES]
            if rope:
                zc = zc * cos_ref[...] + _swap16(zc) * sin_ref[...]
            cols.append(zc)
        return jnp.concatenate(cols, axis=1)

    head_dim = da // (2 * DA_HEADS)
    hd2 = da // DA_HEADS
    q = rotated(0) * (head_dim ** -0.5 * LOG2_E)
    qt_ref[0] = q.T.astype(BF16)
    k = rotated(da).astype(BF16)
    for head in range(DA_HEADS):
        k_ref[0, head] = k[:, head * hd2:(head + 1) * hd2]
    vt_ref[0] = z[:, 2 * da: 3 * da].T.astype(BF16)
    ugp_ref[0] = z[:, 3 * da:]


def _inproj(x, mod, g, w_stack, layer, rope_tabs, tm):
    b, s, d = x.shape
    in_width = w_stack.shape[2]
    da = d // 2
    rest = in_width - 3 * da
    per_batch_mod = mod.shape[0] > 1
    rope = rope_tabs is not None
    in_specs = [
        pl.BlockSpec((1, tm, d), lambda bi, i: (bi, i, 0)),
        pl.BlockSpec((1, 6, d), (lambda bi, i: (bi, 0, 0)) if per_batch_mod else (lambda bi, i: (0, 0, 0))),
        _resident((1, d)),
        _resident_layer(w_stack.shape, layer),
    ]
    args = [x, mod, g, w_stack]
    if rope:
        in_specs += [pl.BlockSpec((tm, LANES), lambda bi, i: (i, 0))] * 2
        args += list(rope_tabs)
    return pl.pallas_call(
        functools.partial(_inproj_body, da, rope),
        grid=(b, s // tm),
        in_specs=in_specs,
        out_specs=[
            pl.BlockSpec((1, da, tm), lambda bi, i: (bi, 0, i)),
            pl.BlockSpec((1, DA_HEADS, tm, da // DA_HEADS), lambda bi, i: (bi, 0, i, 0)),
            pl.BlockSpec((1, da, tm), lambda bi, i: (bi, 0, i)),
            pl.BlockSpec((1, tm, rest), lambda bi, i: (bi, i, 0)),
        ],
        out_shape=[
            jax.ShapeDtypeStruct((b, da, s), BF16),
            jax.ShapeDtypeStruct((b, DA_HEADS, s, da // DA_HEADS), BF16),
            jax.ShapeDtypeStruct((b, da, s), BF16),
            jax.ShapeDtypeStruct((b, s, rest), F32),
        ],
        compiler_params=_params(2),
        name="inproj_rope" if rope else "inproj",
    )(*args)


def _attn_body(n_lat, tk, lam_init, *refs):
    if n_lat:
        (qt_ref, kc_ref, vtc_ref, kl_ref, vtl_ref, lamv_ref, gsub_ref,
         o_ref, vec_ref, vel_ref, m_ref, acc_ref, gap_ref) = refs
    else:
        (qt_ref, kc_ref, vtc_ref, lamv_ref, gsub_ref,
         o_ref, vec_ref, m_ref, acc_ref) = refs
    hd2 = qt_ref.shape[1]
    tq = qt_ref.shape[2]

    @pl.when(pl.program_id(2) == 0)
    def _():
        vec_ref[0:hd2, :] = vtc_ref[0]
        vec_ref[hd2:, :] = jnp.ones((BF16_ROWS, vec_ref.shape[1]), BF16)
        if n_lat:
            for t in range(n_lat):
                vel_ref[t, 0:hd2, :] = vtl_ref[0, :, t * tk:(t + 1) * tk]
                vel_ref[t, hd2:, :] = jnp.ones((BF16_ROWS, tk), BF16)

    qt = qt_ref[0]
    row = lax.broadcasted_iota(jnp.int32, qt.shape, 0)
    zero = jnp.zeros_like(qt)
    q_comp = (jnp.where(row < hd2 // 2, qt, zero), jnp.where(row >= hd2 // 2, qt, zero))

    def first_block(k_blk, ve_blk, j):
        st = jnp.dot(k_blk, q_comp[j], preferred_element_type=F32)
        bm = jnp.max(st, axis=0, keepdims=True)
        pt = jnp.exp2(st - bm).astype(BF16)
        acc_ref[j] = jnp.dot(ve_blk, pt, preferred_element_type=F32)
        m_ref[j, 0:1, :] = bm

    def exact_block(k_blk, ve_blk, j):
        st = jnp.dot(k_blk, q_comp[j], preferred_element_type=F32)
        m_prev = m_ref[j, 0:1, :]
        m_new = jnp.maximum(m_prev, jnp.max(st, axis=0, keepdims=True))
        pt = jnp.exp2(st - m_new).astype(BF16)
        acc_ref[j] = (jnp.exp2(m_prev - m_new) * acc_ref[j]
                      + jnp.dot(ve_blk, pt, preferred_element_type=F32))
        m_ref[j, 0:1, :] = m_new

    def lazy_block(k_blk, ve_blk, j):
        st = jnp.dot(k_blk, q_comp[j], preferred_element_type=F32)
        m_prev = m_ref[j, 0:1, :]
        pt = jnp.exp2(st - m_prev).astype(BF16)
        bm = jnp.max(st, axis=0, keepdims=True)
        m_new = jnp.maximum(m_prev, bm)
        acc_ref[j] = jnp.exp2(m_prev - m_new) * (acc_ref[j] + jnp.dot(ve_blk, pt, preferred_element_type=F32))
        gap_ref[j, 0:1, :] = jnp.maximum(gap_ref[j, 0:1, :], bm - m_prev)
        gap_ref[j, 1:2, :] = jnp.maximum(gap_ref[j, 1:2, :], bm)
        m_ref[j, 0:1, :] = m_new

    def finalize():
        lamv = lamv_ref[...]
        lam = (jnp.exp(jnp.sum(lamv[0:1] * lamv[1:2], axis=-1, keepdims=True))
               - jnp.exp(jnp.sum(lamv[2:3] * lamv[3:4], axis=-1, keepdims=True)) + lam_init)
        a0 = acc_ref[0]
        a1 = acc_ref[1]
        o = a0[0:hd2] * (1.0 / a0[hd2:hd2 + 1]) - lam * (a1[0:hd2] * (1.0 / a1[hd2:hd2 + 1]))
        ms = jnp.mean(o * o, axis=0, keepdims=True)
        on = o * lax.rsqrt(ms + EPS) * gsub_ref[...] * (1.0 - lam_init)
        o_ref[0] = on.T.astype(o_ref.dtype)

    def sweep(block):
        for j in range(2):
            first_block(kc_ref[0, 0], vec_ref[...], j)
        if n_lat:
            def body(t, carry):
                start = pl.multiple_of(t * tk, tk)
                k_blk = kl_ref[0, 0, pl.ds(start, tk), :]
                for j in range(2):
                    block(k_blk, vel_ref[t], j)
                return carry
            lax.fori_loop(0, n_lat, body, 0)

    if n_lat:
        gap_ref[:, 0:1, :] = jnp.zeros((2, 1, tq), F32)
        gap_ref[:, 1:2, :] = jnp.full((2, 1, tq), -jnp.inf, F32)
        m_ref[...] = jnp.zeros(m_ref.shape, F32)
        acc_ref[...] = jnp.zeros(acc_ref.shape, F32)
        for t in range(n_lat):
            for j in range(2):
                lazy_block(kl_ref[0, 0, t * tk:(t + 1) * tk, :], vel_ref[t], j)
            if t == (n_lat - 1) // 2:
                for j in range(2):
                    lazy_block(kc_ref[0, 0], vec_ref[...], j)
        worst_rise = jnp.max(jnp.maximum(gap_ref[0, 0:1, :], gap_ref[1, 0:1, :]))
        lowest_max = jnp.min(jnp.minimum(gap_ref[0, 1:2, :], gap_ref[1, 1:2, :]))
        valid = jnp.logical_and(worst_rise <= MAX_LAZY_GAP, lowest_max >= -MAX_LAZY_GAP)
        finalize()

        @pl.when(jnp.logical_not(valid))
        def _():
            sweep(exact_block)
            finalize()
    else:
        sweep(None)
        finalize()


def _attention(qt, k_c, vt_c, k_l, vt_l, lamv, gsub_col, lam_init, tq, tk):
    b, da, sq = qt.shape
    hd2 = da // DA_HEADS
    sc = k_c.shape[2]
    latent = k_l is not None
    n_lat = k_l.shape[2] // tk if latent else 0
    in_specs = [
        pl.BlockSpec((1, hd2, tq), lambda bi, h, i: (bi, h, i)),
        pl.BlockSpec((1, 1, sc, hd2), lambda bi, h, i: (bi, h, 0, 0)),
        pl.BlockSpec((1, hd2, sc), lambda bi, h, i: (bi, h, 0)),
    ]
    args = [qt, k_c, vt_c]
    scratch = [pltpu.VMEM((hd2 + BF16_ROWS, sc), BF16)]
    if latent:
        sl = k_l.shape[2]
        in_specs += [
            pl.BlockSpec((1, 1, sl, hd2), lambda bi, h, i: (bi, h, 0, 0)),
            pl.BlockSpec((1, hd2, sl), lambda bi, h, i: (bi, h, 0)),
        ]
        args += [k_l, vt_l]
        scratch.append(pltpu.VMEM((n_lat, hd2 + BF16_ROWS, tk), BF16))
    in_specs += [_resident(lamv.shape), _resident(gsub_col.shape)]
    args += [lamv, gsub_col]
    scratch += [pltpu.VMEM((2, SUBLANES, tq), F32), pltpu.VMEM((2, hd2 + BF16_ROWS, tq), F32)]
    if latent:
        scratch.append(pltpu.VMEM((2, SUBLANES, tq), F32))
    return pl.pallas_call(
        functools.partial(_attn_body, n_lat, tk, lam_init),
        grid=(b, DA_HEADS, sq // tq),
        in_specs=in_specs,
        out_specs=pl.BlockSpec((1, tq, hd2), lambda bi, h, i: (bi, i, h)),
        out_shape=jax.ShapeDtypeStruct((b, sq, da), BF16),
        scratch_shapes=scratch,
        compiler_params=_params(3),
        name="attn_latent" if latent else "attn_ctx",
    )(*args)


def _merge_body(seq_len, a_ref, u_ref, gv_ref, p_ref, pprev_ref, pnext_ref, x_ref, mod_ref,
                gvn_ref, ws_ref, bs_ref, gones_ref, wpool_ref, spool_ref, wout_ref, o_ref):
    i = pl.program_id(1)
    tm = u_ref.shape[1]
    cm = u_ref.shape[2]
    da = a_ref.shape[2]
    gdim = cm // CM_GROUPS

    o = jnp.dot(a_ref[0], wout_ref[0:da, :], preferred_element_type=F32)

    gv = gv_ref[0]
    sq = gv * gv
    sq_hi = sq.astype(BF16)
    sq_lo = (sq - sq_hi.astype(F32)).astype(BF16)
    gms = (jnp.dot(sq_hi, gones_ref[...], preferred_element_type=F32)
           + jnp.dot(sq_lo, gones_ref[...], preferred_element_type=F32))
    vn = gv * lax.rsqrt(gms + EPS) * gvn_ref[...]
    grp_chunk = lax.broadcasted_iota(jnp.int32, (CHUNK, cm), 1) // gdim
    chunks = []
    for c in range(tm // CHUNK):
        vn_c = vn[c * CHUNK:(c + 1) * CHUNK, :]
        stacked = jnp.concatenate(
            [jnp.where(grp_chunk == g, vn_c, 0.0).astype(BF16) for g in range(CM_GROUPS)], axis=0)
        chunks.append(jnp.dot(ws_ref[...], stacked, preferred_element_type=F32) + bs_ref[...])
    m_b = u_ref[0] * jnp.concatenate(chunks, axis=0)
    o = o + jnp.dot(m_b.astype(BF16), wout_ref[da:da + cm, :], preferred_element_type=F32)

    pc = p_ref[0]
    pprev = jnp.where(i == 0, 0.0, pprev_ref[0])
    pnext = jnp.where(i == pl.num_programs(1) - 1, 0.0, pnext_ref[0])
    pe = jnp.concatenate([pprev, pc, pnext], axis=0)
    lane = lax.broadcasted_iota(jnp.int32, (tm, LANES), 1)
    lane8 = lax.broadcasted_iota(jnp.int32, (POOL_HALO, LANES), 1)
    row8 = lax.broadcasted_iota(jnp.int32, (POOL_HALO, LANES), 0)
    is_first = i == 0
    is_last = i == pl.num_programs(1) - 1
    means = []
    for col in range(cm // LANES):
        groups = range(col * LANES // gdim, (col + 1) * LANES // gdim)
        trailing = pe[:, col * LANES:(col + 1) * LANES]
        wsum = None
        half8 = None
        for g in range(groups[-1] + 1):
            w = POOL_WINDOWS[g]
            assert w == 2 ** (g + 1) and w // 2 <= POOL_HALO
            trailing = trailing + pltpu.roll(trailing, w // 2, axis=0)
            if g in groups:
                last_row = POOL_HALO + w - w // 2 - 1
                centred = trailing[last_row:last_row + tm, :]
                in_g = (lane // gdim) == (g - groups[0])
                wsum = centred if wsum is None else jnp.where(in_g, centred, wsum)
                in_g8 = (lane8 // gdim) == (g - groups[0])
                half8 = jnp.full((POOL_HALO, LANES), w // 2, jnp.int32) if half8 is None else jnp.where(in_g8, w // 2, half8)
        mean = wsum * (0.5 / half8[0:1, :].astype(F32))

        def clipped_mean(pos8, wsum8):
            cnt = jnp.minimum(pos8 + half8, seq_len) - jnp.maximum(pos8 - half8, 0)
            return wsum8 / cnt.astype(F32)

        top = jnp.where(is_first, clipped_mean(row8, wsum[0:POOL_HALO]), mean[0:POOL_HALO])
        bottom = jnp.where(is_last, clipped_mean(seq_len - POOL_HALO + row8, wsum[tm - POOL_HALO:]),
                           mean[tm - POOL_HALO:])
        means.append(jnp.concatenate([top, mean[POOL_HALO:tm - POOL_HALO], bottom], axis=0))
    d_pool = (jnp.concatenate(means, axis=1) - pc).astype(BF16)
    m_c = jnp.dot(d_pool, wpool_ref[...], preferred_element_type=F32) * spool_ref[...]
    o = o + jnp.dot(m_c.astype(BF16), wout_ref[da + cm:, :], preferred_element_type=F32)

    o_ref[0] = x_ref[0] + mod_ref[0, 2:3, :] * o


def _merge(a, ugp, x, mod, gvn, ws_bf16, bs_tab, gones, wpool_bd, spool, wout_stack, layer, tm):
    b, s, d = x.shape
    da = a.shape[2]
    cm = ugp.shape[2] // 3
    per_batch_mod = mod.shape[0] > 1
    hb = tm // POOL_HALO
    n_hb = s // POOL_HALO
    in_specs = [
        pl.BlockSpec((1, tm, da), lambda bi, i: (bi, i, 0)),
        pl.BlockSpec((1, tm, cm), lambda bi, i: (bi, i, 0)),
        pl.BlockSpec((1, tm, cm), lambda bi, i: (bi, i, 1)),
        pl.BlockSpec((1, tm, cm), lambda bi, i: (bi, i, 2)),
        pl.BlockSpec((1, POOL_HALO, cm), lambda bi, i: (bi, jnp.maximum(i * hb - 1, 0), 2)),
        pl.BlockSpec((1, POOL_HALO, cm), lambda bi, i: (bi, jnp.minimum((i + 1) * hb, n_hb - 1), 2)),
        pl.BlockSpec((1, tm, d), lambda bi, i: (bi, i, 0)),
        pl.BlockSpec((1, 6, d), (lambda bi, i: (bi, 0, 0)) if per_batch_mod else (lambda bi, i: (0, 0, 0))),
        _resident(gvn.shape), _resident(ws_bf16.shape), _resident(bs_tab.shape), _resident(gones.shape),
        _resident(wpool_bd.shape), _resident(spool.shape), _resident_layer(wout_stack.shape, layer),
    ]
    return pl.pallas_call(
        functools.partial(_merge_body, s),
        grid=(b, s // tm),
        in_specs=in_specs,
        out_specs=pl.BlockSpec((1, tm, d), lambda bi, i: (bi, i, 0)),
        out_shape=jax.ShapeDtypeStruct((b, s, d), F32),
        compiler_params=_params(2),
        name="merge",
    )(a, ugp, ugp, ugp, ugp, ugp, x, mod, gvn, ws_bf16, bs_tab, gones, wpool_bd, spool, wout_stack)


def _mlp_body(tf, final, x_ref, mod_ref, g_ref, w1_ref, w2_ref, *rest):
    if final:
        gf_ref, o_ref = rest
    else:
        (o_ref,) = rest
    x = x_ref[0]
    shift = mod_ref[0, 3:4, :]
    scale = mod_ref[0, 4:5, :]
    h = (_rmsnorm_rows(x, g_ref[...]) * (1.0 + scale) + shift).astype(BF16)
    acc = jnp.zeros(x.shape, F32)
    for f in range(w1_ref.shape[1] // tf):
        t = jnp.dot(h, w1_ref[:, f * tf:(f + 1) * tf], preferred_element_type=F32)
        t = jnp.square(jnp.maximum(t, 0.0)).astype(BF16)
        acc = acc + jnp.dot(t, w2_ref[f * tf:(f + 1) * tf, :], preferred_element_type=F32)
    y = x + mod_ref[0, 5:6, :] * acc
    if final:
        y = _rmsnorm_rows(y, gf_ref[...])
    o_ref[0] = y


def _mlp(x, mod, g, w1_stack, w2_stack, layer, g_final, tm, tf):
    b, s, d = x.shape
    per_batch_mod = mod.shape[0] > 1
    final = g_final is not None
    in_specs = [
        pl.BlockSpec((1, tm, d), lambda bi, i: (bi, i, 0)),
        pl.BlockSpec((1, 6, d), (lambda bi, i: (bi, 0, 0)) if per_batch_mod else (lambda bi, i: (0, 0, 0))),
        _resident(g.shape), _resident_layer(w1_stack.shape, layer), _resident_layer(w2_stack.shape, layer),
    ]
    args = [x, mod, g, w1_stack, w2_stack]
    if final:
        in_specs.append(_resident(g_final.shape))
        args.append(g_final)
    return pl.pallas_call(
        functools.partial(_mlp_body, tf, final),
        grid=(b, s // tm),
        in_specs=in_specs,
        out_specs=pl.BlockSpec((1, tm, d), lambda bi, i: (bi, i, 0)),
        out_shape=jax.ShapeDtypeStruct((b, s, d), F32),
        compiler_params=_params(2),
        name="mlp_final" if final else "mlp",
    )(*args)


def _rope_tables(n_tokens, head_dim):
    n_rows = n_tokens // GRID_W
    row = jnp.repeat(jnp.arange(n_rows), GRID_W).astype(F32)
    col = jnp.tile(jnp.arange(GRID_W), n_rows).astype(F32)
    n_freq = head_dim // 4
    inv = ROPE_BASE ** (-jnp.arange(n_freq, dtype=F32) / n_freq)
    cr, sr = jnp.cos(row[:, None] * inv), jnp.sin(row[:, None] * inv)
    cc, sc = jnp.cos(col[:, None] * inv), jnp.sin(col[:, None] * inv)
    cos64 = jnp.concatenate([cr, cr, cc, cc], axis=1)
    sin64 = jnp.concatenate([-sr, sr, -sc, sc], axis=1)
    reps = LANES // head_dim
    return jnp.tile(cos64, (1, reps)), jnp.tile(sin64, (1, reps))


def _block_diag(blocks):
    g, r, c = blocks.shape
    out = jnp.zeros((g * r, g * c), blocks.dtype)
    for i in range(g):
        out = out.at[i * r:(i + 1) * r, i * c:(i + 1) * c].set(blocks[i])
    return out


def kernel(x, c, ctx, c_ctx, w_ada, b_ada, g_norm_mix, g_norm_mlp, w_in, lam_q1, lam_k1, lam_q2,
           lam_k2, g_subln, g_vnorm, w_spatial, b_spatial, w_pool, s_pool, w_out, w1, w2, g_final):
    n_layers = w_ada.shape[0]
    batch, seq, d = x.shape
    ctx_len = ctx.shape[1]
    da = d // 2
    head_dim = da // (2 * DA_HEADS)
    cm = d // 4
    gdim = cm // CM_GROUPS

    tm_l, tm_c = min(TM_LATENT, seq), ctx_len
    tq_l, tq_c = min(TQ_LATENT, seq), ctx_len
    tk = min(TK_LATENT, seq)
    tf = TF_MLP

    rows = -(-(batch + 1) // SUBLANES) * SUBLANES
    conds = jnp.zeros((rows, d), F32).at[:batch].set(c).at[batch].set(c_ctx)
    mods = _ada_all_layers(conds, w_ada, b_ada)
    rope_tabs = _rope_tables(seq, head_dim)
    gones = _block_diag(jnp.full((CM_GROUPS, gdim, gdim), 1.0 / gdim, F32)).astype(BF16)

    w_in_b, w_out_b, w1_b, w2_b = (w.astype(BF16) for w in (w_in, w_out, w1, w2))

    xl, xc = x, ctx
    for l in range(n_layers):
        last = l == n_layers - 1
        lam_init = 0.8 - 0.6 * math.exp(-0.3 * l)
        mod_l = mods[l, :batch].reshape(batch, 6, d)
        mod_c = mods[l, batch].reshape(1, 6, d)
        g_mix = g_norm_mix[l][None]
        lamv = jnp.stack([lam_q1[l], lam_k1[l], lam_q2[l], lam_k2[l]]).astype(F32)
        gsub_col = g_subln[l][:, None]
        merge_w = (g_vnorm[l][None],
                   jnp.transpose(w_spatial[l], (1, 0, 2)).reshape(CHUNK, CM_GROUPS * CHUNK).astype(BF16),
                   jnp.repeat(b_spatial[l].T, gdim, axis=1), gones,
                   _block_diag(w_pool[l]).astype(BF16), s_pool[l][None], w_out_b, l)
        g_mlp = g_norm_mlp[l][None]

        qt_l, k_l, vt_l, ugp_l = _inproj(xl, mod_l, g_mix, w_in_b, l, rope_tabs, tm_l)
        qt_c, k_c, vt_c, ugp_c = _inproj(xc, mod_c, g_mix, w_in_b, l, None, tm_c)
        a_l = _attention(qt_l, k_c, vt_c, k_l, vt_l, lamv, gsub_col, lam_init, tq_l, tk)
        xl = _merge(a_l, ugp_l, xl, mod_l, *merge_w, tm_l)
        xl = _mlp(xl, mod_l, g_mlp, w1_b, w2_b, l, g_final[None] if last else None, tm_l, tf)
        if not last:
            a_c = _attention(qt_c, k_c, vt_c, None, None, lamv, gsub_col, lam_init, tq_c, tk)
            xc = _merge(a_c, ugp_c, xc, mod_c, *merge_w, tm_c)
            xc = _mlp(xc, mod_c, g_mlp, w1_b, w2_b, l, None, tm_c, tf)
    return xl
```

```python
import functools
import math

import jax
import jax.numpy as jnp
from jax import lax
from jax.experimental import pallas as pl
from jax.experimental.pallas import tpu as pltpu

EPS = 1e-6
GRID_W = 64
ROPE_BASE = 10000.0
DA_HEADS = 4
POOL_WINDOWS = (2, 4, 8, 16)
CM_GROUPS = 4
CHUNK = 128

LANES = 128
SUBLANES = 8
BF16_ROWS = 16
VMEM_LIMIT = 56 * 1024 * 1024
POOL_HALO = 8

TM_LATENT = 1024
TQ_LATENT = 2048
TK_LATENT = 512
TF_MLP = 1024

LOG2_E = math.log2(math.e)
MAX_LAZY_GAP = 64.0

F32 = jnp.float32
BF16 = jnp.bfloat16


def _params(n_axes):
    return pltpu.CompilerParams(
        dimension_semantics=("arbitrary",) * n_axes, vmem_limit_bytes=VMEM_LIMIT)


def _resident(shape):
    zeros = (0,) * len(shape)
    return pl.BlockSpec(shape, lambda *_: zeros, pipeline_mode=pl.Buffered(1))


def _resident_layer(stack_shape, layer):
    zeros = (0,) * (len(stack_shape) - 1)
    return pl.BlockSpec((None,) + tuple(stack_shape[1:]), lambda *_: (layer,) + zeros,
                        pipeline_mode=pl.Buffered(1))


def _rmsnorm_rows(x, g):
    ms = jnp.mean(x * x, axis=-1, keepdims=True)
    return x * lax.rsqrt(ms + EPS) * g


def _ada_body(cond_ref, w_ref, b_ref, o_ref):
    c = cond_ref[...]
    s = c * (1.0 / (1.0 + jnp.exp(-c)))
    o_ref[0] = jnp.dot(s, w_ref[0], preferred_element_type=F32,
                       precision=lax.Precision.HIGHEST) + b_ref[0]


def _ada_all_layers(conds, w_ada, b_ada):
    n_layers, d, n_out = w_ada.shape
    rows = conds.shape[0]
    tn = n_out // 4
    return pl.pallas_call(
        _ada_body,
        grid=(n_layers, n_out // tn),
        in_specs=[
            pl.BlockSpec((rows, d), lambda l, j: (0, 0)),
            pl.BlockSpec((1, d, tn), lambda l, j: (l, 0, j)),
            pl.BlockSpec((1, 1, tn), lambda l, j: (l, 0, j)),
        ],
        out_specs=pl.BlockSpec((1, rows, tn), lambda l, j: (l, 0, j)),
        out_shape=jax.ShapeDtypeStruct((n_layers, rows, n_out), F32),
        compiler_params=_params(2),
        name="ada",
    )(conds, w_ada, b_ada.reshape(n_layers, 1, n_out))


def _swap16(z):
    up = pltpu.roll(z, LANES - 16, axis=1)
    down = pltpu.roll(z, 16, axis=1)
    lane = lax.broadcasted_iota(jnp.int32, z.shape, 1)
    return jnp.where((lane & 16) == 0, up, down)


def _inproj_body(da, rope, x_ref, mod_ref, g_ref, w_ref, *rest):
    if rope:
        cos_ref, sin_ref, qt_ref, k_ref, vt_ref, ugp_ref = rest
    else:
        qt_ref, k_ref, vt_ref, ugp_ref = rest
    x = x_ref[0]
    shift = mod_ref[0, 0:1, :]
    scale = mod_ref[0, 1:2, :]
    h = (_rmsnorm_rows(x, g_ref[...]) * (1.0 + scale) + shift).astype(BF16)
    z = jnp.dot(h, w_ref[...], preferred_element_type=F32)

    def rotated(lo):
        cols = []
        for c in range(da // LANES):
            zc = z[:, lo + c * LANES: lo + (c + 1) * LANES]
            if rope:
                zc = zc * cos_ref[...] + _swap16(zc) * sin_ref[...]
            cols.append(zc)
        return jnp.concatenate(cols, axis=1)

    head_dim = da // (2 * DA_HEADS)
    hd2 = da // DA_HEADS
    q = rotated(0) * (head_dim ** -0.5 * LOG2_E)
    qt_ref[0] = q.T.astype(BF16)
    k = rotated(da).astype(BF16)
    for head in range(DA_HEADS):
        k_ref[0, head] = k[:, head * hd2:(head + 1) * hd2]
    vt_ref[0] = z[:, 2 * da: 3 * da].T.astype(BF16)
    ugp_ref[0] = z[:, 3 * da:]


def _inproj(x, mod, g, w_stack, layer, rope_tabs, tm):
    b, s, d = x.shape
    in_width = w_stack.shape[2]
    da = d // 2
    rest = in_width - 3 * da
    per_batch_mod = mod.shape[0] > 1
    rope = rope_tabs is not None
    in_specs = [
        pl.BlockSpec((1, tm, d), lambda bi, i: (bi, i, 0)),
        pl.BlockSpec((1, 6, d), (lambda bi, i: (bi, 0, 0)) if per_batch_mod else (lambda bi, i: (0, 0, 0))),
        _resident((1, d)),
        _resident_layer(w_stack.shape, layer),
    ]
    args = [x, mod, g, w_stack]
    if rope:
        in_specs += [pl.BlockSpec((tm, LANES), lambda bi, i: (i, 0))] * 2
        args += list(rope_tabs)
    return pl.pallas_call(
        functools.partial(_inproj_body, da, rope),
        grid=(b, s // tm),
        in_specs=in_specs,
        out_specs=[
            pl.BlockSpec((1, da, tm), lambda bi, i: (bi, 0, i)),
            pl.BlockSpec((1, DA_HEADS, tm, da // DA_HEADS), lambda bi, i: (bi, 0, i, 0)),
            pl.BlockSpec((1, da, tm), lambda bi, i: (bi, 0, i)),
            pl.BlockSpec((1, tm, rest), lambda bi, i: (bi, i, 0)),
        ],
        out_shape=[
            jax.ShapeDtypeStruct((b, da, s), BF16),
            jax.ShapeDtypeStruct((b, DA_HEADS, s, da // DA_HEADS), BF16),
            jax.ShapeDtypeStruct((b, da, s), BF16),
            jax.ShapeDtypeStruct((b, s, rest), F32),
        ],
        compiler_params=_params(2),
        name="inproj_rope" if rope else "inproj",
    )(*args)


def _attn_body(n_lat, tk, lam_init, *refs):
    if n_lat:
        (qt_ref, kc_ref, vtc_ref, kl_ref, vtl_ref, lamv_ref, gsub_ref,
         o_ref, vec_ref, vel_ref, m_ref, acc_ref, gap_ref) = refs
    else:
        (qt_ref, kc_ref, vtc_ref, lamv_ref, gsub_ref,
         o_ref, vec_ref, m_ref, acc_ref) = refs
    hd2 = qt_ref.shape[1]
    tq = qt_ref.shape[2]

    @pl.when(pl.program_id(2) == 0)
    def _():
        vec_ref[0:hd2, :] = vtc_ref[0]
        vec_ref[hd2:, :] = jnp.ones((BF16_ROWS, vec_ref.shape[1]), BF16)
        if n_lat:
            for t in range(n_lat):
                vel_ref[t, 0:hd2, :] = vtl_ref[0, :, t * tk:(t + 1) * tk]
                vel_ref[t, hd2:, :] = jnp.ones((BF16_ROWS, tk), BF16)

    qt = qt_ref[0]
    row = lax.broadcasted_iota(jnp.int32, qt.shape, 0)
    zero = jnp.zeros_like(qt)
    q_comp = (jnp.where(row < hd2 // 2, qt, zero), jnp.where(row >= hd2 // 2, qt, zero))

    def first_block(k_blk, ve_blk, j):
        st = jnp.dot(k_blk, q_comp[j], preferred_element_type=F32)
        bm = jnp.max(st, axis=0, keepdims=True)
        pt = jnp.exp2(st - bm).astype(BF16)
        acc_ref[j] = jnp.dot(ve_blk, pt, preferred_element_type=F32)
        m_ref[j, 0:1, :] = bm

    def exact_block(k_blk, ve_blk, j):
        st = jnp.dot(k_blk, q_comp[j], preferred_element_type=F32)
        m_prev = m_ref[j, 0:1, :]
        m_new = jnp.maximum(m_prev, jnp.max(st, axis=0, keepdims=True))
        pt = jnp.exp2(st - m_new).astype(BF16)
        acc_ref[j] = (jnp.exp2(m_prev - m_new) * acc_ref[j]
                      + jnp.dot(ve_blk, pt, preferred_element_type=F32))
        m_ref[j, 0:1, :] = m_new

    def lazy_block(k_blk, ve_blk, j):
        st = jnp.dot(k_blk, q_comp[j], preferred_element_type=F32)
        m_prev = m_ref[j, 0:1, :]
        pt = jnp.exp2(st - m_prev).astype(BF16)
        bm = jnp.max(st, axis=0, keepdims=True)
        m_new = jnp.maximum(m_prev, bm)
        acc_ref[j] = jnp.exp2(m_prev - m_new) * (acc_ref[j] + jnp.dot(ve_blk, pt, preferred_element_type=F32))
        gap_ref[j, 0:1, :] = jnp.maximum(gap_ref[j, 0:1, :], bm - m_prev)
        gap_ref[j, 1:2, :] = jnp.maximum(gap_ref[j, 1:2, :], bm)
        m_ref[j, 0:1, :] = m_new

    def finalize():
        lamv = lamv_ref[...]
        lam = (jnp.exp(jnp.sum(lamv[0:1] * lamv[1:2], axis=-1, keepdims=True))
               - jnp.exp(jnp.sum(lamv[2:3] * lamv[3:4], axis=-1, keepdims=True)) + lam_init)
        a0 = acc_ref[0]
        a1 = acc_ref[1]
        o = a0[0:hd2] * (1.0 / a0[hd2:hd2 + 1]) - lam * (a1[0:hd2] * (1.0 / a1[hd2:hd2 + 1]))
        ms = jnp.mean(o * o, axis=0, keepdims=True)
        on = o * lax.rsqrt(ms + EPS) * gsub_ref[...] * (1.0 - lam_init)
        o_ref[0] = on.T.astype(o_ref.dtype)

    def sweep(block):
        for j in range(2):
            first_block(kc_ref[0, 0], vec_ref[...], j)
        if n_lat:
            def body(t, carry):
                start = pl.multiple_of(t * tk, tk)
                k_blk = kl_ref[0, 0, pl.ds(start, tk), :]
                for j in range(2):
                    block(k_blk, vel_ref[t], j)
                return carry
            lax.fori_loop(0, n_lat, body, 0)

    if n_lat:
        gap_ref[:, 0:1, :] = jnp.zeros((2, 1, tq), F32)
        gap_ref[:, 1:2, :] = jnp.full((2, 1, tq), -jnp.inf, F32)
        m_ref[...] = jnp.zeros(m_ref.shape, F32)
        acc_ref[...] = jnp.zeros(acc_ref.shape, F32)
        for t in range(n_lat):
            for j in range(2):
                lazy_block(kl_ref[0, 0, t * tk:(t + 1) * tk, :], vel_ref[t], j)
            if t == (n_lat - 1) // 2:
                for j in range(2):
                    lazy_block(kc_ref[0, 0], vec_ref[...], j)
        worst_rise = jnp.max(jnp.maximum(gap_ref[0, 0:1, :], gap_ref[1, 0:1, :]))
        lowest_max = jnp.min(jnp.minimum(gap_ref[0, 1:2, :], gap_ref[1, 1:2, :]))
        valid = jnp.logical_and(worst_rise <= MAX_LAZY_GAP, lowest_max >= -MAX_LAZY_GAP)
        finalize()

        @pl.when(jnp.logical_not(valid))
        def _():
            sweep(exact_block)
            finalize()
    else:
        sweep(None)
        finalize()


def _attn_heads_body(tk, heads, lam_init, qt_ref, kc_ref, vtc_ref, lamv_ref, gsub_ref, o_ref,
                     vec_ref, m_ref, acc_ref):
    hd2 = qt_ref.shape[1] // heads
    for hh in range(heads):
        rows = pl.ds(hh * hd2, hd2)
        _attn_body(0, tk, lam_init, qt_ref.at[:, rows, :], kc_ref.at[:, pl.ds(hh, 1)], vtc_ref.at[:, rows, :],
                   lamv_ref, gsub_ref, o_ref.at[:, :, rows], vec_ref.at[hh], m_ref.at[hh], acc_ref.at[hh])


def _attention(qt, k_c, vt_c, k_l, vt_l, lamv, gsub_col, lam_init, tq, tk):
    b, da, sq = qt.shape
    hd2 = da // DA_HEADS
    sc = k_c.shape[2]
    latent = k_l is not None
    n_lat = k_l.shape[2] // tk if latent else 0
    heads = 1 if latent else DA_HEADS
    in_specs = [
        pl.BlockSpec((1, heads * hd2, tq), lambda bi, h, i: (bi, h, i)),
        pl.BlockSpec((1, heads, sc, hd2), lambda bi, h, i: (bi, h, 0, 0)),
        pl.BlockSpec((1, heads * hd2, sc), lambda bi, h, i: (bi, h, 0)),
    ]
    args = [qt, k_c, vt_c]
    scratch = [pltpu.VMEM((hd2 + BF16_ROWS, sc), BF16) if latent else pltpu.VMEM((heads, hd2 + BF16_ROWS, sc), BF16)]
    if latent:
        sl = k_l.shape[2]
        in_specs += [
            pl.BlockSpec((1, 1, sl, hd2), lambda bi, h, i: (bi, h, 0, 0)),
            pl.BlockSpec((1, hd2, sl), lambda bi, h, i: (bi, h, 0)),
        ]
        args += [k_l, vt_l]
        scratch.append(pltpu.VMEM((n_lat, hd2 + BF16_ROWS, tk), BF16))
    in_specs += [_resident(lamv.shape), _resident(gsub_col.shape)]
    args += [lamv, gsub_col]
    if latent:
        scratch += [pltpu.VMEM((2, SUBLANES, tq), F32), pltpu.VMEM((2, hd2 + BF16_ROWS, tq), F32),
                    pltpu.VMEM((2, SUBLANES, tq), F32)]
        body = functools.partial(_attn_body, n_lat, tk, lam_init)
    else:
        scratch += [pltpu.VMEM((heads, 2, SUBLANES, tq), F32), pltpu.VMEM((heads, 2, hd2 + BF16_ROWS, tq), F32)]
        body = functools.partial(_attn_heads_body, tk, heads, lam_init)
    return pl.pallas_call(
        body,
        grid=(b, DA_HEADS // heads, sq // tq),
        in_specs=in_specs,
        out_specs=pl.BlockSpec((1, tq, heads * hd2), lambda bi, h, i: (bi, i, h)),
        out_shape=jax.ShapeDtypeStruct((b, sq, da), BF16),
        scratch_shapes=scratch,
        compiler_params=_params(3),
        name="attn_latent" if latent else "attn_ctx",
    )(*args)


def _merge_body(seq_len, a_ref, u_ref, gv_ref, p_ref, pprev_ref, pnext_ref, x_ref, mod_ref,
                gvn_ref, ws_ref, bs_ref, gones_ref, wpool_ref, spool_ref, wout_ref, o_ref):
    i = pl.program_id(1)
    tm = u_ref.shape[1]
    cm = u_ref.shape[2]
    da = a_ref.shape[2]
    gdim = cm // CM_GROUPS

    o = jnp.dot(a_ref[0], wout_ref[0:da, :], preferred_element_type=F32)

    gv = gv_ref[0]
    sq = gv * gv
    sq_hi = sq.astype(BF16)
    sq_lo = (sq - sq_hi.astype(F32)).astype(BF16)
    gms = (jnp.dot(sq_hi, gones_ref[...], preferred_element_type=F32)
           + jnp.dot(sq_lo, gones_ref[...], preferred_element_type=F32))
    vn = gv * lax.rsqrt(gms + EPS) * gvn_ref[...]
    grp_chunk = lax.broadcasted_iota(jnp.int32, (CHUNK, cm), 1) // gdim
    chunks = []
    for c in range(tm // CHUNK):
        vn_c = vn[c * CHUNK:(c + 1) * CHUNK, :]
        stacked = jnp.concatenate(
            [jnp.where(grp_chunk == g, vn_c, 0.0).astype(BF16) for g in range(CM_GROUPS)], axis=0)
        chunks.append(jnp.dot(ws_ref[...], stacked, preferred_element_type=F32) + bs_ref[...])
    m_b = u_ref[0] * jnp.concatenate(chunks, axis=0)
    o = o + jnp.dot(m_b.astype(BF16), wout_ref[da:da + cm, :], preferred_element_type=F32)

    pc = p_ref[0]
    pprev = jnp.where(i == 0, 0.0, pprev_ref[0])
    pnext = jnp.where(i == pl.num_programs(1) - 1, 0.0, pnext_ref[0])
    pe = jnp.concatenate([pprev, pc, pnext], axis=0)
    lane = lax.broadcasted_iota(jnp.int32, (tm, LANES), 1)
    lane8 = lax.broadcasted_iota(jnp.int32, (POOL_HALO, LANES), 1)
    row8 = lax.broadcasted_iota(jnp.int32, (POOL_HALO, LANES), 0)
    is_first = i == 0
    is_last = i == pl.num_programs(1) - 1
    means = []
    for col in range(cm // LANES):
        groups = range(col * LANES // gdim, (col + 1) * LANES // gdim)
        trailing = pe[:, col * LANES:(col + 1) * LANES]
        wsum = None
        half8 = None
        for g in range(groups[-1] + 1):
            w = POOL_WINDOWS[g]
            assert w == 2 ** (g + 1) and w // 2 <= POOL_HALO
            trailing = trailing + pltpu.roll(trailing, w // 2, axis=0)
            if g in groups:
                last_row = POOL_HALO + w - w // 2 - 1
                centred = trailing[last_row:last_row + tm, :]
                in_g = (lane // gdim) == (g - groups[0])
                wsum = centred if wsum is None else jnp.where(in_g, centred, wsum)
                in_g8 = (lane8 // gdim) == (g - groups[0])
                half8 = jnp.full((POOL_HALO, LANES), w // 2, jnp.int32) if half8 is None else jnp.where(in_g8, w // 2, half8)
        mean = wsum * (0.5 / half8[0:1, :].astype(F32))

        def clipped_mean(pos8, wsum8):
            cnt = jnp.minimum(pos8 + half8, seq_len) - jnp.maximum(pos8 - half8, 0)
            return wsum8 / cnt.astype(F32)

        top = jnp.where(is_first, clipped_mean(row8, wsum[0:POOL_HALO]), mean[0:POOL_HALO])
        bottom = jnp.where(is_last, clipped_mean(seq_len - POOL_HALO + row8, wsum[tm - POOL_HALO:]),
                           mean[tm - POOL_HALO:])
        means.append(jnp.concatenate([top, mean[POOL_HALO:tm - POOL_HALO], bottom], axis=0))
    d_pool = (jnp.concatenate(means, axis=1) - pc).astype(BF16)
    m_c = jnp.dot(d_pool, wpool_ref[...], preferred_element_type=F32) * spool_ref[...]
    o = o + jnp.dot(m_c.astype(BF16), wout_ref[da + cm:, :], preferred_element_type=F32)

    o_ref[0] = x_ref[0] + mod_ref[0, 2:3, :] * o


def _merge(a, ugp, x, mod, gvn, ws_bf16, bs_tab, gones, wpool_bd, spool, wout_stack, layer, tm):
    b, s, d = x.shape
    da = a.shape[2]
    cm = ugp.shape[2] // 3
    per_batch_mod = mod.shape[0] > 1
    hb = tm // POOL_HALO
    n_hb = s // POOL_HALO
    in_specs = [
        pl.BlockSpec((1, tm, da), lambda bi, i: (bi, i, 0)),
        pl.BlockSpec((1, tm, cm), lambda bi, i: (bi, i, 0)),
        pl.BlockSpec((1, tm, cm), lambda bi, i: (bi, i, 1)),
        pl.BlockSpec((1, tm, cm), lambda bi, i: (bi, i, 2)),
        pl.BlockSpec((1, POOL_HALO, cm), lambda bi, i: (bi, jnp.maximum(i * hb - 1, 0), 2)),
        pl.BlockSpec((1, POOL_HALO, cm), lambda bi, i: (bi, jnp.minimum((i + 1) * hb, n_hb - 1), 2)),
        pl.BlockSpec((1, tm, d), lambda bi, i: (bi, i, 0)),
        pl.BlockSpec((1, 6, d), (lambda bi, i: (bi, 0, 0)) if per_batch_mod else (lambda bi, i: (0, 0, 0))),
        _resident(gvn.shape), _resident(ws_bf16.shape), _resident(bs_tab.shape), _resident(gones.shape),
        _resident(wpool_bd.shape), _resident(spool.shape), _resident_layer(wout_stack.shape, layer),
    ]
    return pl.pallas_call(
        functools.partial(_merge_body, s),
        grid=(b, s // tm),
        in_specs=in_specs,
        out_specs=pl.BlockSpec((1, tm, d), lambda bi, i: (bi, i, 0)),
        out_shape=jax.ShapeDtypeStruct((b, s, d), F32),
        compiler_params=_params(2),
        name="merge",
    )(a, ugp, ugp, ugp, ugp, ugp, x, mod, gvn, ws_bf16, bs_tab, gones, wpool_bd, spool, wout_stack)


def _mlp_body(tf, final, x_ref, mod_ref, g_ref, w1_ref, w2_ref, *rest):
    if final:
        gf_ref, o_ref = rest
    else:
        (o_ref,) = rest
    x = x_ref[0]
    shift = mod_ref[0, 3:4, :]
    scale = mod_ref[0, 4:5, :]
    h = (_rmsnorm_rows(x, g_ref[...]) * (1.0 + scale) + shift).astype(BF16)
    acc = jnp.zeros(x.shape, F32)
    for f in range(w1_ref.shape[1] // tf):
        t = jnp.dot(h, w1_ref[:, f * tf:(f + 1) * tf], preferred_element_type=F32)
        t = jnp.square(jnp.maximum(t, 0.0)).astype(BF16)
        acc = acc + jnp.dot(t, w2_ref[f * tf:(f + 1) * tf, :], preferred_element_type=F32)
    y = x + mod_ref[0, 5:6, :] * acc
    if final:
        y = _rmsnorm_rows(y, gf_ref[...])
    o_ref[0] = y


def _mlp(x, mod, g, w1_stack, w2_stack, layer, g_final, tm, tf):
    b, s, d = x.shape
    per_batch_mod = mod.shape[0] > 1
    final = g_final is not None
    in_specs = [
        pl.BlockSpec((1, tm, d), lambda bi, i: (bi, i, 0)),
        pl.BlockSpec((1, 6, d), (lambda bi, i: (bi, 0, 0)) if per_batch_mod else (lambda bi, i: (0, 0, 0))),
        _resident(g.shape), _resident_layer(w1_stack.shape, layer), _resident_layer(w2_stack.shape, layer),
    ]
    args = [x, mod, g, w1_stack, w2_stack]
    if final:
        in_specs.append(_resident(g_final.shape))
        args.append(g_final)
    return pl.pallas_call(
        functools.partial(_mlp_body, tf, final),
        grid=(b, s // tm),
        in_specs=in_specs,
        out_specs=pl.BlockSpec((1, tm, d), lambda bi, i: (bi, i, 0)),
        out_shape=jax.ShapeDtypeStruct((b, s, d), F32),
        compiler_params=_params(2),
        name="mlp_final" if final else "mlp",
    )(*args)


def _rope_tables(n_tokens, head_dim):
    n_rows = n_tokens // GRID_W
    row = jnp.repeat(jnp.arange(n_rows), GRID_W).astype(F32)
    col = jnp.tile(jnp.arange(GRID_W), n_rows).astype(F32)
    n_freq = head_dim // 4
    inv = ROPE_BASE ** (-jnp.arange(n_freq, dtype=F32) / n_freq)
    cr, sr = jnp.cos(row[:, None] * inv), jnp.sin(row[:, None] * inv)
    cc, sc = jnp.cos(col[:, None] * inv), jnp.sin(col[:, None] * inv)
    cos64 = jnp.concatenate([cr, cr, cc, cc], axis=1)
    sin64 = jnp.concatenate([-sr, sr, -sc, sc], axis=1)
    reps = LANES // head_dim
    return jnp.tile(cos64, (1, reps)), jnp.tile(sin64, (1, reps))


def _block_diag(blocks):
    g, r, c = blocks.shape
    out = jnp.zeros((g * r, g * c), blocks.dtype)
    for i in range(g):
        out = out.at[i * r:(i + 1) * r, i * c:(i + 1) * c].set(blocks[i])
    return out


def kernel(x, c, ctx, c_ctx, w_ada, b_ada, g_norm_mix, g_norm_mlp, w_in, lam_q1, lam_k1, lam_q2,
           lam_k2, g_subln, g_vnorm, w_spatial, b_spatial, w_pool, s_pool, w_out, w1, w2, g_final):
    n_layers = w_ada.shape[0]
    batch, seq, d = x.shape
    ctx_len = ctx.shape[1]
    da = d // 2
    head_dim = da // (2 * DA_HEADS)
    cm = d // 4
    gdim = cm // CM_GROUPS

    tm_l, tm_c = min(TM_LATENT, seq), ctx_len
    tq_l, tq_c = min(TQ_LATENT, seq), ctx_len
    tk = min(TK_LATENT, seq)
    tf = TF_MLP

    rows = -(-(batch + 1) // SUBLANES) * SUBLANES
    conds = jnp.zeros((rows, d), F32).at[:batch].set(c).at[batch].set(c_ctx)
    mods = _ada_all_layers(conds, w_ada, b_ada)
    rope_tabs = _rope_tables(seq, head_dim)
    gones = _block_diag(jnp.full((CM_GROUPS, gdim, gdim), 1.0 / gdim, F32)).astype(BF16)

    w_in_b, w_out_b, w1_b, w2_b = (w.astype(BF16) for w in (w_in, w_out, w1, w2))

    xl, xc = x, ctx
    for l in range(n_layers):
        last = l == n_layers - 1
        lam_init = 0.8 - 0.6 * math.exp(-0.3 * l)
        mod_l = mods[l, :batch].reshape(batch, 6, d)
        mod_c = mods[l, batch].reshape(1, 6, d)
        g_mix = g_norm_mix[l][None]
        lamv = jnp.stack([lam_q1[l], lam_k1[l], lam_q2[l], lam_k2[l]]).astype(F32)
        gsub_col = g_subln[l][:, None]
        merge_w = (g_vnorm[l][None],
                   jnp.transpose(w_spatial[l], (1, 0, 2)).reshape(CHUNK, CM_GROUPS * CHUNK).astype(BF16),
                   jnp.repeat(b_spatial[l].T, gdim, axis=1), gones,
                   _block_diag(w_pool[l]).astype(BF16), s_pool[l][None], w_out_b, l)
        g_mlp = g_norm_mlp[l][None]

        qt_l, k_l, vt_l, ugp_l = _inproj(xl, mod_l, g_mix, w_in_b, l, rope_tabs, tm_l)
        qt_c, k_c, vt_c, ugp_c = _inproj(xc, mod_c, g_mix, w_in_b, l, None, tm_c)
        a_l = _attention(qt_l, k_c, vt_c, k_l, vt_l, lamv, gsub_col, lam_init, tq_l, tk)
        xl = _merge(a_l, ugp_l, xl, mod_l, *merge_w, tm_l)
        xl = _mlp(xl, mod_l, g_mlp, w1_b, w2_b, l, g_final[None] if last else None, tm_l, tf)
        if not last:
            a_c = _attention(qt_c, k_c, vt_c, None, None, lamv, gsub_col, lam_init, tq_c, tk)
            xc = _merge(a_c, ugp_c, xc, mod_c, *merge_w, tm_c)
            xc = _mlp(xc, mod_c, g_mlp, w1_b, w2_b, l, None, tm_c, tf)
    return xl
```
